```python
import math, functools
import jax
import jax.numpy as jnp
from jax import lax
import numpy as np

D_MODEL = 1024
BATCH = 8
SEQ = 4096
DEPTH = 1
DEC_BATCH = 128
DEC_SEQ = 8
PAST_LEN = 8192
PAGE_SIZE = 128

N_RET_HEADS = 4
RET_DK = D_MODEL // 8
RET_DV = D_MODEL // 4
RET_CHUNK = 128
N_DIFF_HEADS = 8
DIFF_HD = D_MODEL // 16
DIFF_DV = 2 * DIFF_HD
D_FF = 128 * ((8 * D_MODEL // 3 + 127) // 128)
ROPE_THETA = 10000.0
NORM_EPS = 1e-6
Q_BLOCK = 128
NEG_INF = -1e30
IN_WIDTHS = (N_RET_HEADS * RET_DK, N_RET_HEADS * RET_DK, N_RET_HEADS * RET_DV, N_RET_HEADS * RET_DV,
             N_DIFF_HEADS * 2 * DIFF_HD, N_DIFF_HEADS * 2 * DIFF_HD, N_DIFF_HEADS * DIFF_DV, D_MODEL, D_MODEL)
IN_WIDTH = sum(IN_WIDTHS)

kernel_name = 'hybrid_retention_diffattn_decode_step'


def rms_norm(x, g):
    xf = x.astype(jnp.float32)
    y = xf * lax.rsqrt(jnp.mean(xf * xf, axis=-1, keepdims=True) + NORM_EPS)
    return (y * g.astype(jnp.float32)).astype(x.dtype)


def rotary(x, pos):
    d = x.shape[-1]
    half = d // 2
    inv = ROPE_THETA ** (-jnp.arange(half, dtype=jnp.float32) / half)
    ang = pos[:, None] * inv[None, :]
    cos = jnp.cos(ang)[None, :, None, :]
    sin = jnp.sin(ang)[None, :, None, :]
    xf = x.astype(jnp.float32)
    x1, x2 = xf[..., :half], xf[..., half:]
    return jnp.concatenate([x1 * cos - x2 * sin, x2 * cos + x1 * sin], axis=-1).astype(x.dtype)


def swiglu_half(x, g, w1, w3, w2):
    h = rms_norm(x, g)
    return x + 0.5 * ((jax.nn.silu(h @ w1) * (h @ w3)) @ w2)


def lambda_init(layer):
    return 0.8 - 0.6 * math.exp(-0.3 * layer)


def mixer_inputs(h, pos, w_in, q_norm, k_norm):
    B, L, _ = h.shape
    z = h @ w_in
    splits = np.cumsum(IN_WIDTHS)[:-1].tolist()
    rq, rk, rv, rg, dq, dk, dv, gr, gd = jnp.split(z, splits, axis=-1)
    rq = rotary(rq.reshape(B, L, N_RET_HEADS, RET_DK), pos)
    rk = rotary(rk.reshape(B, L, N_RET_HEADS, RET_DK), pos) * (RET_DK ** -0.5)
    rv = rv.reshape(B, L, N_RET_HEADS, RET_DV)
    dq = rotary(rms_norm(dq.reshape(B, L, 2 * N_DIFF_HEADS, DIFF_HD), q_norm), pos)
    dk = rotary(rms_norm(dk.reshape(B, L, 2 * N_DIFF_HEADS, DIFF_HD), k_norm), pos)
    dq = dq.reshape(B, L, N_DIFF_HEADS, 2, DIFF_HD)
    dk = dk.reshape(B, L, N_DIFF_HEADS, 2, DIFF_HD)
    dv = dv.reshape(B, L, N_DIFF_HEADS, DIFF_DV)
    return rq, rk, rv, rg, dq, dk, dv, gr, gd


def retention(q, k, v, s0, chunk):
    B, L, H, dk = q.shape
    dv = v.shape[-1]
    n = L // chunk
    lg = jnp.log1p(-(2.0 ** (-5.0 - jnp.arange(H, dtype=jnp.float32))))
    idx = jnp.arange(chunk, dtype=jnp.float32)
    diff = idx[:, None] - idx[None, :]
    dmask = jnp.where(diff >= 0, jnp.exp(jnp.maximum(diff, 0.0)[None] * lg[:, None, None]), 0.0)
    q_dec = jnp.exp((idx[:, None] + 1.0) * lg[None, :])
    k_dec = jnp.exp((chunk - 1.0 - idx)[:, None] * lg[None, :])
    c_dec = jnp.exp(chunk * lg)

    def to_chunks(a):
        return a.astype(jnp.float32).reshape(B, n, chunk, H, a.shape[-1]).transpose(1, 0, 2, 3, 4)

    def step(s, xs):
        qc, kc, vc = xs
        att = jnp.einsum('bihd,bjhd->bhij', qc, kc) * dmask[None]
        inner = jnp.einsum('bhij,bjhv->bihv', att, vc)
        cross = jnp.einsum('bihd,bhdv->bihv', qc, s) * q_dec[None, :, :, None]
        s = s * c_dec[None, :, None, None] + jnp.einsum('bjhd,bjhv->bhdv', kc * k_dec[None, :, :, None], vc)
        return s, inner + cross

    s, o = lax.scan(step, s0.astype(jnp.float32), (to_chunks(q), to_chunks(k), to_chunks(v)))
    o = o.transpose(1, 0, 2, 3, 4).reshape(B, L, H, dv)
    return o, s


def retention_out(o, g, norm_g, w_proj):
    B, L, H, dv = o.shape
    o = rms_norm(o, norm_g).reshape(B, L, H * dv).astype(g.dtype)
    return (jax.nn.silu(g) * o) @ w_proj


def diff_attn_prompt(q, k, v, lam):
    B, L, H, _, d = q.shape
    qb = min(Q_BLOCK, L)
    nb = L // qb
    kf = k.astype(jnp.float32)
    vf = v.astype(jnp.float32)
    q_blocks = (q.astype(jnp.float32) * d ** -0.5).reshape(B, nb, qb, H, 2, d).transpose(1, 0, 2, 3, 4, 5)
    starts = jnp.arange(nb, dtype=jnp.int32) * qb
    kpos = jnp.arange(L, dtype=jnp.int32)

    def one_block(args):
        qc, start = args
        s = jnp.einsum('bqhmd,bkhmd->bhmqk', qc, kf)
        qpos = start + jnp.arange(qb, dtype=jnp.int32)
        mask = kpos[None, :] <= qpos[:, None]
        p = jax.nn.softmax(jnp.where(mask, s, NEG_INF), axis=-1)
        o = jnp.einsum('bhmqk,bkhv->bqhmv', p, vf)
        return o[:, :, :, 0] - lam * o[:, :, :, 1]

    o = lax.map(one_block, (q_blocks, starts))
    return o.transpose(1, 0, 2, 3, 4).reshape(B, L, H, v.shape[-1])


def online_update(state, s, v):
    m, l, acc = state
    m_new = jnp.maximum(m, jnp.max(s, axis=-1))
    corr = jnp.exp(m - m_new)
    p = jnp.exp(s - m_new[..., None])
    l = l * corr + jnp.sum(p, axis=-1)
    acc = acc * corr[..., None] + jnp.einsum('bhmqk,bkhv->bhmqv', p, v)
    return (m_new, l, acc)


def diff_attn_sample(q, k, v, lam, cache_k, cache_v, page_table):
    B, T, H, _, d = q.shape
    dv = v.shape[-1]
    page = cache_k.shape[1]
    qf = q.astype(jnp.float32) * d ** -0.5

    def page_step(state, phys):
        kp = cache_k[phys].astype(jnp.float32).reshape(B, page, H, 2, d)
        vp = cache_v[phys].astype(jnp.float32)
        s = jnp.einsum('bqhmd,bkhmd->bhmqk', qf, kp)
        return online_update(state, s, vp), None

    init = (jnp.full((B, H, 2, T), NEG_INF, jnp.float32),
            jnp.zeros((B, H, 2, T), jnp.float32),
            jnp.zeros((B, H, 2, T, dv), jnp.float32))
    state, _ = lax.scan(page_step, init, page_table.T)
    s = jnp.einsum('bqhmd,bkhmd->bhmqk', qf, k.astype(jnp.float32))
    causal = jnp.tril(jnp.ones((T, T), dtype=bool))
    state = online_update(state, jnp.where(causal, s, NEG_INF), v.astype(jnp.float32))
    _, l, acc = state
    o = acc / l[..., None]
    o = o[:, :, 0] - lam * o[:, :, 1]
    return o.transpose(0, 2, 1, 3)


def diff_out(o, norm_g, lam_i, w_proj, dtype):
    B, L, H, dv = o.shape
    o = (rms_norm(o, norm_g) * (1.0 - lam_i)).reshape(B, L, H * dv).astype(dtype)
    return o @ w_proj


def setup_inputs(seed: int = 0) -> dict:
    key = jax.random.key(seed)
    ks = jax.random.split(key, 32)
    f32 = jnp.float32
    n_pages = PAST_LEN // PAGE_SIZE
    n_pool = (DEC_BATCH * n_pages * 5) // 4

    def w(k, shape, fan_in):
        return jax.random.normal(k, shape, f32) * fan_in ** -0.5

    def gain(k, shape):
        return 1.0 + 0.02 * jax.random.normal(k, shape, f32)

    perm = jax.random.permutation(ks[5], n_pool)
    page_table = perm[: DEC_BATCH * n_pages].reshape(DEC_BATCH, n_pages).astype(jnp.int32)
    return {
        'x_prompt': jax.random.normal(ks[0], (BATCH, SEQ, D_MODEL), f32),
        'x_sample': jax.random.normal(ks[1], (DEC_BATCH, DEC_SEQ, D_MODEL), f32),
        'cache_k': jax.random.normal(ks[2], (DEPTH, n_pool, PAGE_SIZE, N_DIFF_HEADS, 2 * DIFF_HD), f32),
        'cache_v': jax.random.normal(ks[3], (DEPTH, n_pool, PAGE_SIZE, N_DIFF_HEADS, DIFF_DV), f32),
        'state_ret': 0.5 * jax.random.normal(ks[4], (DEPTH, DEC_BATCH, N_RET_HEADS, RET_DK, RET_DV), f32),
        'page_table': page_table,
        'norm_ffn1': gain(ks[6], (DEPTH, D_MODEL)),
        'ffn1_w1': w(ks[7], (DEPTH, D_MODEL, D_FF), D_MODEL),
        'ffn1_w3': w(ks[8], (DEPTH, D_MODEL, D_FF), D_MODEL),
        'ffn1_w2': w(ks[9], (DEPTH, D_FF, D_MODEL), D_FF),
        'norm_mix': gain(ks[10], (DEPTH, D_MODEL)),
        'w_in': w(ks[11], (DEPTH, D_MODEL, IN_WIDTH), D_MODEL),
        'ret_norm': gain(ks[12], (DEPTH, N_RET_HEADS, RET_DV)),
        'w_ret_proj': w(ks[13], (DEPTH, N_RET_HEADS * RET_DV, D_MODEL), N_RET_HEADS * RET_DV),
        'q_norm': gain(ks[14], (DEPTH, DIFF_HD)),
        'k_norm': gain(ks[15], (DEPTH, DIFF_HD)),
        'lambda_q1': 0.1 * jax.random.normal(ks[16], (DEPTH, DIFF_HD), f32),
        'lambda_k1': 0.1 * jax.random.normal(ks[17], (DEPTH, DIFF_HD), f32),
        'lambda_q2': 0.1 * jax.random.normal(ks[18], (DEPTH, DIFF_HD), f32),
        'lambda_k2': 0.1 * jax.random.normal(ks[19], (DEPTH, DIFF_HD), f32),
        'diff_norm': gain(ks[20], (DEPTH, DIFF_DV)),
        'w_diff_proj': w(ks[21], (DEPTH, N_DIFF_HEADS * DIFF_DV, D_MODEL), N_DIFF_HEADS * DIFF_DV),
        'w_o': w(ks[22], (DEPTH, D_MODEL, D_MODEL), D_MODEL),
        'norm_ffn2': gain(ks[23], (DEPTH, D_MODEL)),
        'ffn2_w1': w(ks[24], (DEPTH, D_MODEL, D_FF), D_MODEL),
        'ffn2_w3': w(ks[25], (DEPTH, D_MODEL, D_FF), D_MODEL),
        'ffn2_w2': w(ks[26], (DEPTH, D_FF, D_MODEL), D_FF),
    }


def reference(x_prompt, x_sample, cache_k, cache_v, state_ret, page_table,
              norm_ffn1, ffn1_w1, ffn1_w3, ffn1_w2, norm_mix, w_in, ret_norm, w_ret_proj,
              q_norm, k_norm, lambda_q1, lambda_k1, lambda_q2, lambda_k2, diff_norm, w_diff_proj,
              w_o, norm_ffn2, ffn2_w1, ffn2_w3, ffn2_w2):
    B, S, _ = x_prompt.shape
    DB, T, _ = x_sample.shape
    past_len = page_table.shape[1] * cache_k.shape[2]
    pos_p = jnp.arange(S, dtype=jnp.float32)
    pos_s = past_len + jnp.arange(T, dtype=jnp.float32)
    prompt_chunk = RET_CHUNK if S % RET_CHUNK == 0 else S

    def run_layer(l, x, pos, s0, chunk, attend):
        x = swiglu_half(x, norm_ffn1[l], ffn1_w1[l], ffn1_w3[l], ffn1_w2[l])
        h = rms_norm(x, norm_mix[l])
        rq, rk, rv, rg, dq, dk, dv, gr, gd = mixer_inputs(h, pos, w_in[l], q_norm[l], k_norm[l])
        o_ret, s_new = retention(rq, rk, rv, s0, chunk)
        lam_i = lambda_init(l)
        lam = (jnp.exp(jnp.sum(lambda_q1[l].astype(jnp.float32) * lambda_k1[l].astype(jnp.float32)))
               - jnp.exp(jnp.sum(lambda_q2[l].astype(jnp.float32) * lambda_k2[l].astype(jnp.float32)))
               + lam_i)
        o_diff = attend(dq, dk, dv, lam)
        y_ret = retention_out(o_ret, rg, ret_norm[l], w_ret_proj[l])
        y_diff = diff_out(o_diff, diff_norm[l], lam_i, w_diff_proj[l], x.dtype)
        merged = jax.nn.sigmoid(gr) * y_ret + jax.nn.sigmoid(gd) * y_diff
        x = x + merged @ w_o[l]
        x = swiglu_half(x, norm_ffn2[l], ffn2_w1[l], ffn2_w3[l], ffn2_w2[l])
        k_rows = dk.reshape(x.shape[0], x.shape[1], N_DIFF_HEADS, 2 * DIFF_HD)
        return x, k_rows, dv, s_new.astype(x.dtype)

    xp, xs = x_prompt, x_sample
    kp_l, vp_l, sp_l, ks_l, vs_l, ss_l = [], [], [], [], [], []
    for l in range(DEPTH):
        s0_p = jnp.zeros((B, N_RET_HEADS, RET_DK, RET_DV), jnp.float32)
        xp, kp, vp, sp = run_layer(l, xp, pos_p, s0_p, prompt_chunk, diff_attn_prompt)
        attend_s = functools.partial(diff_attn_sample, cache_k=cache_k[l], cache_v=cache_v[l], page_table=page_table)
        xs, kss, vss, sss = run_layer(l, xs, pos_s, state_ret[l], T, attend_s)
        kp_l.append(kp)
        vp_l.append(vp)
        sp_l.append(sp)
        ks_l.append(kss)
        vs_l.append(vss)
        ss_l.append(sss)
    return (xp, xs, jnp.stack(kp_l), jnp.stack(vp_l), jnp.stack(sp_l), jnp.stack(ks_l), jnp.stack(vs_l), jnp.stack(ss_l))
```

```python
import functools
import math

import numpy as np
import jax
import jax.numpy as jnp
from jax import lax
from jax.experimental import pallas as pl
from jax.experimental.pallas import tpu as pltpu

F32 = jnp.float32
BF16 = jnp.bfloat16

N_RET_HEADS = 4
N_DIFF_HEADS = 8
RET_CHUNK = 128
ROPE_THETA = 10000.0
NORM_EPS = 1e-6
NEG_INF = -1e30

LANES = 128
MXU_DIM = 256
VMEM_LIMIT = 56 * 1024 * 1024

FFN_TM = 512
FFN_CHUNK = MXU_DIM
MIX_TM = 256
MERGE_TM = 512
FLASH_TQ = 256
FLASH_TK = 512
PAGES_PER_STEP = 4
DEC_BLOCK = 16

_NT = (((1,), (1,)), ((), ()))
_TN = (((0,), (0,)), ((), ()))


def _cparams(*sem):
    return pltpu.CompilerParams(dimension_semantics=sem, vmem_limit_bytes=VMEM_LIMIT)


def _resident(shape):
    nd = len(shape)
    return pl.BlockSpec(shape, lambda *_: (0,) * nd, pipeline_mode=pl.Buffered(1))


def _silu(x):
    return x * jax.nn.sigmoid(x)


def _ffn_kernel(x_ref, g_ref, w1_ref, w3_ref, w2_ref, o_ref, h_ref, acc_ref, *, n_chunks):
    x = x_ref[...]
    r = lax.rsqrt(jnp.mean(x * x, axis=-1, keepdims=True) + NORM_EPS)
    h_ref[...] = (x * r * g_ref[...]).astype(BF16)
    acc_ref[...] = jnp.zeros_like(acc_ref)

    def body(c, carry):
        h = h_ref[...]
        a = jnp.dot(h, w1_ref[c], preferred_element_type=F32)
        b = jnp.dot(h, w3_ref[c], preferred_element_type=F32)
        u = (_silu(a) * b).astype(BF16)
        acc_ref[...] += jnp.dot(u, w2_ref[c], preferred_element_type=F32)
        return carry

    lax.fori_loop(0, n_chunks, body, 0)
    o_ref[...] = x_ref[...] + 0.5 * acc_ref[...]


def _ffn_half(x, g, w1c, w3c, w2c):
    n, d = x.shape
    nc, _, fc = w1c.shape
    tm = min(FFN_TM, n)
    row = pl.BlockSpec((tm, d), lambda i: (i, 0))
    return pl.pallas_call(
        functools.partial(_ffn_kernel, n_chunks=nc),
        grid=(n // tm,),
        in_specs=[row, _resident((1, d)), _resident((nc, d, fc)), _resident((nc, d, fc)),
                  _resident((nc, fc, d))],
        out_specs=row,
        out_shape=jax.ShapeDtypeStruct((n, d), F32),
        scratch_shapes=[pltpu.VMEM((tm, d), BF16), pltpu.VMEM((tm, d), F32)],
        compiler_params=_cparams("parallel"),
        name="ffn_half",
    )(x, g, w1c, w3c, w2c)


def _mix_kernel(x_ref, g_ref, w_ref, cosr_ref, sinr_ref, cosd_ref, sina_ref, sinb_ref, qg_ref, kg_ref,
                bd_ref, rq_o, rk_o, rv_o, rg_o, dq_o, dk_o, dkb_o, dv_o, dvb_o, gr_o, gd_o, *, d, hd):
    x = x_ref[...]
    r = lax.rsqrt(jnp.mean(x * x, axis=-1, keepdims=True) + NORM_EPS)
    h = (x * r * g_ref[...]).astype(BF16)

    def proj(c0, width):
        return jnp.dot(h, w_ref[:, c0:c0 + width], preferred_element_type=F32)

    rw = d // 2
    cosr, sinr = cosr_ref[...], sinr_ref[...]
    for o_ref, c0, scale in ((rq_o, 0, None), (rk_o, rw, LANES ** -0.5)):
        z = proj(c0, rw)
        for hh in range(rw // LANES):
            zh = z[:, hh * LANES:(hh + 1) * LANES]
            y = zh * cosr + pltpu.roll(zh, LANES // 2, 1) * sinr
            if scale is not None:
                y = y * scale
            o_ref[:, hh * LANES:(hh + 1) * LANES] = y.astype(o_ref.dtype)

    rv_o[...] = proj(2 * rw, d).astype(rv_o.dtype)
    rg_o[...] = proj(2 * rw + d, d).astype(rg_o.dtype)

    cosd, sina, sinb = cosd_ref[...], sina_ref[...], sinb_ref[...]
    bd = bd_ref[...]
    c_dq = 2 * rw + 2 * d
    for c0, gain_ref, is_q in ((c_dq, qg_ref, True), (c_dq + d, kg_ref, False)):
        z = proj(c0, d)
        gain = gain_ref[...]
        for cc in range(d // MXU_DIM):
            zc = z[:, cc * MXU_DIM:(cc + 1) * MXU_DIM]
            ss = jnp.dot((zc * zc).astype(BF16), bd, preferred_element_type=F32)
            yn = zc * lax.rsqrt(ss * (1.0 / hd) + NORM_EPS)
            for s2 in range(MXU_DIM // LANES):
                y = yn[:, s2 * LANES:(s2 + 1) * LANES] * gain
                y = (y * cosd + pltpu.roll(y, LANES - hd // 2, 1) * sina
                     + pltpu.roll(y, hd // 2, 1) * sinb)
                col = cc * MXU_DIM + s2 * LANES
                if is_q:
                    dq_o[:, col:col + LANES] = (y * hd ** -0.5).astype(dq_o.dtype)
                else:
                    dk_o[:, col:col + LANES] = y
                    dkb_o[:, col:col + LANES] = y.astype(dkb_o.dtype)

    z = proj(c_dq + 2 * d, d)
    dv_o[...] = z
    dvb_o[...] = z.astype(dvb_o.dtype)
    gr_o[...] = jax.nn.sigmoid(proj(c_dq + 3 * d, d)).astype(gr_o.dtype)
    gd_o[...] = jax.nn.sigmoid(proj(c_dq + 4 * d, d)).astype(gd_o.dtype)


def _mixer_in(x, g, w_in, tabs, qg, kg, bd, act_dtype, hd):
    n, d = x.shape
    tm = min(MIX_TM, n)
    n_tab = tabs[0].shape[0] // tm
    row = lambda w: pl.BlockSpec((tm, w), lambda i: (i, 0))
    tab = pl.BlockSpec((tm, LANES), lambda i: (i % n_tab, 0))
    sds = lambda w, dt: jax.ShapeDtypeStruct((n, w), dt)
    widths = (d // 2, d // 2, d, d, d, d, d, d, d, d, d)
    dtypes = (act_dtype, act_dtype, act_dtype, act_dtype, act_dtype, F32, act_dtype, F32, act_dtype,
              act_dtype, act_dtype)
    return pl.pallas_call(
        functools.partial(_mix_kernel, d=d, hd=hd),
        grid=(n // tm,),
        in_specs=[row(d), _resident((1, d)), _resident(w_in.shape), tab, tab, tab, tab, tab,
                  _resident((1, LANES)), _resident((1, LANES)), _resident((MXU_DIM, MXU_DIM))],
        out_specs=[row(w) for w in widths],
        out_shape=[sds(w, dt) for w, dt in zip(widths, dtypes)],
        compiler_params=_cparams("parallel"),
        name="mixer_in",
    )(x, g, w_in, *tabs, qg, kg, bd)


def _ret_out(o, gate, norm_g):
    r = lax.rsqrt(jnp.mean(o * o, axis=-1, keepdims=True) + NORM_EPS)
    return _silu(gate) * (o * r * norm_g)


def _ret_kernel(q_ref, k_ref, v_ref, g_ref, dm_ref, qd_ref, kd_ref, rn_ref, a_ref, s_ref, *, cdec, dk, dv):
    @pl.when(pl.program_id(1) == 0)
    def _():
        s_ref[...] = jnp.zeros_like(s_ref)

    for h in range(N_RET_HEADS):
        ks = slice(h * dk, (h + 1) * dk)
        vs = slice(h * dv, (h + 1) * dv)
        q, k, v = q_ref[:, ks], k_ref[:, ks], v_ref[:, vs]
        att = lax.dot_general(q, k, _NT, preferred_element_type=F32) * dm_ref[h]
        inner = jnp.dot(att.astype(BF16), v, preferred_element_type=F32)
        s = s_ref[0, h]
        qdec = (q.astype(F32) * qd_ref[:, ks]).astype(BF16)
        cross = jnp.dot(qdec, s.astype(BF16), preferred_element_type=F32)
        kdec = (k.astype(F32) * kd_ref[:, ks]).astype(BF16)
        s_ref[0, h] = s * cdec[h] + lax.dot_general(kdec, v, _TN, preferred_element_type=F32)
        a_ref[:, vs] = _ret_out(inner + cross, g_ref[:, vs].astype(F32), rn_ref[:, vs]).astype(a_ref.dtype)


def _ret_tables(chunk, dk):
    h = np.arange(N_RET_HEADS, dtype=np.float64)
    lg = np.log1p(-(2.0 ** (-5.0 - h)))
    idx = np.arange(chunk, dtype=np.float64)
    diff = idx[:, None] - idx[None, :]
    dmask = np.where(diff >= 0, np.exp(np.maximum(diff, 0.0)[None] * lg[:, None, None]), 0.0)
    q_dec = np.exp((idx[:, None] + 1.0) * lg[None, :])
    k_dec = np.exp((chunk - 1.0 - idx)[:, None] * lg[None, :])
    c_dec = tuple(float(c) for c in np.exp(chunk * lg))
    rep = lambda t: np.repeat(t, dk, axis=1)
    return dmask.astype(np.float32), rep(q_dec).astype(np.float32), rep(k_dec).astype(np.float32), c_dec


def _retention_prompt(rq, rk, rv, rg, ret_norm, batch, seq):
    n, hk = rq.shape
    hv = rv.shape[1]
    dk, dv = hk // N_RET_HEADS, hv // N_RET_HEADS
    chunk = RET_CHUNK if seq % RET_CHUNK == 0 else seq
    nc = seq // chunk
    dmask, q_dec, k_dec, c_dec = _ret_tables(chunk, dk)
    row = lambda w: pl.BlockSpec((chunk, w), lambda b, c: (b * nc + c, 0))
    return pl.pallas_call(
        functools.partial(_ret_kernel, cdec=c_dec, dk=dk, dv=dv),
        grid=(batch, nc),
        in_specs=[row(hk), row(hk), row(hv), row(hv), _resident(dmask.shape), _resident(q_dec.shape),
                  _resident(k_dec.shape), _resident((1, hv))],
        out_specs=[row(hv), pl.BlockSpec((1, N_RET_HEADS, dk, dv), lambda b, c: (b, 0, 0, 0))],
        out_shape=[jax.ShapeDtypeStruct((n, hv), BF16),
                   jax.ShapeDtypeStruct((batch, N_RET_HEADS, dk, dv), F32)],
        compiler_params=_cparams("parallel", "arbitrary"),
        name="retention_prompt",
    )(rq, rk, rv, rg, jnp.asarray(dmask), jnp.asarray(q_dec), jnp.asarray(k_dec), ret_norm)


def _ret_dec_kernel(q_ref, k_ref, v_ref, g_ref, s_ref, dm_ref, qd_ref, kd_ref, rn_ref, a_ref, so_ref,
                    cross_ref, *, cdec, dk, dv, t_len):
    rows = q_ref.shape[0]
    nb = rows // t_len
    row_seq = lax.broadcasted_iota(jnp.int32, (rows, dk), 0) // t_len

    def body(b, carry):
        r0 = pl.multiple_of(b * t_len, t_len)
        for h in range(N_RET_HEADS):
            ks = slice(h * dk, (h + 1) * dk)
            vs = slice(h * dv, (h + 1) * dv)
            s = s_ref[b, h]
            qb = q_ref[pl.ds(r0, t_len), ks] * qd_ref[pl.ds(r0, t_len), ks]
            cross_ref[pl.ds(r0, t_len), vs] = jnp.dot(qb, s, preferred_element_type=F32)
            kb = jnp.where(row_seq == b, k_ref[:, ks] * kd_ref[:, ks], 0.0)
            so_ref[b, h] = s * cdec[h] + lax.dot_general(kb, v_ref[:, vs], _TN, preferred_element_type=F32)
        return carry

    lax.fori_loop(0, nb, body, 0)

    for h in range(N_RET_HEADS):
        ks = slice(h * dk, (h + 1) * dk)
        vs = slice(h * dv, (h + 1) * dv)
        att = lax.dot_general(q_ref[:, ks], k_ref[:, ks], _NT, preferred_element_type=F32) * dm_ref[h]
        inner = jnp.dot(att, v_ref[:, vs], preferred_element_type=F32)
        a_ref[:, vs] = _ret_out(inner + cross_ref[:, vs], g_ref[:, vs], rn_ref[:, vs]).astype(a_ref.dtype)


def _retention_sample(rq, rk, rv, rg, state, ret_norm, t_len):
    n, hk = rq.shape
    hv = rv.shape[1]
    dk, dv = hk // N_RET_HEADS, hv // N_RET_HEADS
    nseq = n // t_len
    bb = min(DEC_BLOCK, nseq)
    rows = bb * t_len
    dmask, q_dec, k_dec, c_dec = _ret_tables(t_len, dk)
    same_seq = np.kron(np.eye(bb, dtype=np.float32), np.ones((t_len, t_len), np.float32))
    dmask = np.tile(dmask, (1, bb, bb)) * same_seq[None]
    q_dec, k_dec = np.tile(q_dec, (bb, 1)), np.tile(k_dec, (bb, 1))
    row = lambda w: pl.BlockSpec((rows, w), lambda i: (i, 0))
    st = pl.BlockSpec((bb, N_RET_HEADS, dk, dv), lambda i: (i, 0, 0, 0))
    return pl.pallas_call(
        functools.partial(_ret_dec_kernel, cdec=c_dec, dk=dk, dv=dv, t_len=t_len),
        grid=(nseq // bb,),
        in_specs=[row(hk), row(hk), row(hv), row(hv), st, _resident(dmask.shape), _resident(q_dec.shape),
                  _resident(k_dec.shape), _resident((1, hv))],
        out_specs=[row(hv), st],
        out_shape=[jax.ShapeDtypeStruct((n, hv), F32), jax.ShapeDtypeStruct(state.shape, F32)],
        scratch_shapes=[pltpu.VMEM((rows, hv), F32)],
        compiler_params=_cparams("parallel"),
        name="retention_sample",
    )(rq, rk, rv, rg, state, jnp.asarray(dmask), jnp.asarray(q_dec), jnp.asarray(k_dec), ret_norm)


def _lambda(lq1_ref, lk1_ref, lq2_ref, lk2_ref, lam_init):
    s1 = jnp.sum(lq1_ref[...] * lk1_ref[...], axis=-1, keepdims=True)
    s2 = jnp.sum(lq2_ref[...] * lk2_ref[...], axis=-1, keepdims=True)
    return jnp.exp(s1) - jnp.exp(s2) + lam_init


def _diff_out(o, norm_g, lam_init):
    r = lax.rsqrt(jnp.mean(o * o, axis=-1, keepdims=True) + NORM_EPS)
    return o * r * norm_g * (1.0 - lam_init)


def _online_block(s, v, m, l, acc):
    m_new = jnp.maximum(m, jnp.max(s, axis=-1, keepdims=True))
    corr = jnp.exp(m - m_new)
    p = jnp.exp(s - m_new)
    l = l * corr + jnp.sum(p, axis=-1, keepdims=True)
    acc = acc * corr + jnp.dot(p.astype(BF16), v, preferred_element_type=F32)
    return m_new, l, acc


def _flash_kernel(q_ref, k_ref, v_ref, lq1_ref, lk1_ref, lq2_ref, lk2_ref, dn_ref, o_ref, qs_ref, *,
                  tq, tk, hd, lam_init):
    i = pl.program_id(2)
    q = q_ref[...]
    lane = lax.broadcasted_iota(jnp.int32, q.shape, 1)
    zero = jnp.zeros_like(q)
    qs_ref[0:tq, :] = jnp.where(lane < hd, q, zero)
    qs_ref[tq:2 * tq, :] = jnp.where(lane >= hd, q, zero)
    qs = qs_ref[...]
    dv = v_ref.shape[1]

    def block(j, carry, masked):
        k0 = pl.multiple_of(j * tk, tk)
        s = lax.dot_general(qs, k_ref[pl.ds(k0, tk), :], _NT, preferred_element_type=F32)
        if masked:
            qpos = i * tq + lax.broadcasted_iota(jnp.int32, s.shape, 0) % tq
            kpos = k0 + lax.broadcasted_iota(jnp.int32, s.shape, 1)
            s = jnp.where(kpos <= qpos, s, NEG_INF)
        return _online_block(s, v_ref[pl.ds(k0, tk), :], *carry)

    init = (jnp.full((2 * tq, 1), NEG_INF, F32), jnp.zeros((2 * tq, 1), F32), jnp.zeros((2 * tq, dv), F32))
    n_full = (i * tq) // tk
    n_all = (i * tq + tq - 1) // tk + 1
    carry = lax.fori_loop(0, n_full, functools.partial(block, masked=False), init)
    _, l, acc = lax.fori_loop(n_full, n_all, functools.partial(block, masked=True), carry)
    o = acc / l
    lam = _lambda(lq1_ref, lk1_ref, lq2_ref, lk2_ref, lam_init)
    o_ref[...] = _diff_out(o[0:tq] - lam * o[tq:2 * tq], dn_ref[...], lam_init).astype(o_ref.dtype)


def _diff_attn_prompt(dq, dk, dv, lam_params, diff_norm, batch, seq, lam_init):
    n, hw = dq.shape
    hd = hw // (2 * N_DIFF_HEADS)
    w = 2 * hd
    tq, tk = min(FLASH_TQ, seq), min(FLASH_TK, seq)
    nq = seq // tq
    qrow = pl.BlockSpec((tq, w), lambda b, h, i: (b * nq + i, h))
    kv = pl.BlockSpec((seq, w), lambda b, h, i: (b, h))
    return pl.pallas_call(
        functools.partial(_flash_kernel, tq=tq, tk=tk, hd=hd, lam_init=lam_init),
        grid=(batch, N_DIFF_HEADS, nq),
        in_specs=[qrow, kv, kv] + [_resident((1, hd))] * 4 + [_resident((1, w))],
        out_specs=qrow,
        out_shape=jax.ShapeDtypeStruct((n, hw), BF16),
        scratch_shapes=[pltpu.VMEM((2 * tq, w), BF16)],
        compiler_params=_cparams("parallel", "parallel", "arbitrary"),
        name="diff_attn_prompt",
    )(dq, dk, dv, *lam_params, diff_norm)


def _paged_kernel(pt_ref, q_ref, kn_ref, vn_ref, lq1_ref, lk1_ref, lq2_ref, lk2_ref, dn_ref, *rest,
                  n_pp, t_len, hd, lam_init):
    k_refs, v_refs = rest[:n_pp], rest[n_pp:2 * n_pp]
    o_ref, qbd_ref, m_ref, l_ref, acc_ref = rest[2 * n_pp:]
    p = pl.program_id(1)
    n_rows, width = qbd_ref.shape

    @pl.when(p == 0)
    def _():
        q = jnp.tile(q_ref[...], (n_rows // t_len, 1))
        r = lax.broadcasted_iota(jnp.int32, q.shape, 0) // t_len
        c = lax.broadcasted_iota(jnp.int32, q.shape, 1) // hd
        qbd_ref[...] = jnp.where(r == c, q, 0.0).astype(BF16)
        m_ref[...] = jnp.full_like(m_ref, NEG_INF)
        l_ref[...] = jnp.zeros_like(l_ref)
        acc_ref[...] = jnp.zeros_like(acc_ref)

    qbd = qbd_ref[...]

    def update(s, pv_fn):
        m_prev = m_ref[...]
        m_new = jnp.maximum(m_prev, jnp.max(s, axis=-1, keepdims=True))
        corr = jnp.exp(m_prev - m_new)
        pe = jnp.exp(s - m_new)
        l_ref[...] = l_ref[...] * corr + jnp.sum(pe, axis=-1, keepdims=True)
        acc_ref[...] = acc_ref[...] * corr + pv_fn(pe.astype(BF16))
        m_ref[...] = m_new

    page = k_refs[0].shape[0]
    s = jnp.concatenate(
        [lax.dot_general(qbd, k_refs[r][...].astype(BF16), _NT, preferred_element_type=F32)
         for r in range(n_pp)], axis=1)

    def pv_pages(pb):
        out = None
        for r in range(n_pp):
            t = jnp.dot(pb[:, r * page:(r + 1) * page], v_refs[r][...].astype(BF16),
                        preferred_element_type=F32)
            out = t if out is None else out + t
        return out

    update(s, pv_pages)

    @pl.when(p == pl.num_programs(1) - 1)
    def _():
        pad = jnp.zeros((LANES - t_len, width), F32)
        kn = jnp.concatenate([kn_ref[...], pad], axis=0).astype(BF16)
        vn = jnp.concatenate([vn_ref[...], pad], axis=0).astype(BF16)
        sn = lax.dot_general(qbd, kn, _NT, preferred_element_type=F32)
        qpos = lax.broadcasted_iota(jnp.int32, sn.shape, 0) % t_len
        kpos = lax.broadcasted_iota(jnp.int32, sn.shape, 1)
        sn = jnp.where(kpos <= qpos, sn, NEG_INF)
        update(sn, lambda pb: jnp.dot(pb, vn, preferred_element_type=F32))
        o = acc_ref[...] / l_ref[...]
        lam = _lambda(lq1_ref, lk1_ref, lq2_ref, lk2_ref, lam_init)
        dvw = width // N_DIFF_HEADS
        for h in range(N_DIFF_HEADS):
            cs = slice(h * dvw, (h + 1) * dvw)
            o1 = o[(2 * h) * t_len:(2 * h + 1) * t_len, cs]
            o2 = o[(2 * h + 1) * t_len:(2 * h + 2) * t_len, cs]
            o_ref[:, cs] = _diff_out(o1 - lam * o2, dn_ref[...], lam_init).astype(o_ref.dtype)


def _diff_attn_sample(dq, dk, dv, cache_k, cache_v, page_table, lam_params, diff_norm, t_len, lam_init):
    n, hw = dq.shape
    hd = hw // (2 * N_DIFF_HEADS)
    nseq, n_pages = page_table.shape
    n_pool, page = cache_k.shape[0], cache_k.shape[1]
    ck = cache_k.reshape(n_pool, page, hw)
    cv = cache_v.reshape(n_pool, page, hw)
    n_pp = math.gcd(PAGES_PER_STEP, n_pages)
    n_rows = 2 * N_DIFF_HEADS * t_len
    row = pl.BlockSpec((t_len, hw), lambda b, p, pt: (b, 0))
    const = lambda shape: pl.BlockSpec(shape, lambda b, p, pt: (0, 0))
    page_spec = lambda r: pl.BlockSpec((None, page, hw), lambda b, p, pt: (pt[b, p * n_pp + r], 0, 0))
    grid_spec = pltpu.PrefetchScalarGridSpec(
        num_scalar_prefetch=1,
        grid=(nseq, n_pages // n_pp),
        in_specs=[row, row, row] + [const((1, hd))] * 4 + [const((1, 2 * hd))]
        + [page_spec(r) for r in range(n_pp)] * 2,
        out_specs=row,
        scratch_shapes=[pltpu.VMEM((n_rows, hw), BF16), pltpu.VMEM((n_rows, 1), F32),
                        pltpu.VMEM((n_rows, 1), F32), pltpu.VMEM((n_rows, hw), F32)],
    )
    return pl.pallas_call(
        functools.partial(_paged_kernel, n_pp=n_pp, t_len=t_len, hd=hd, lam_init=lam_init),
        grid_spec=grid_spec,
        out_shape=jax.ShapeDtypeStruct((n, hw), F32),
        compiler_params=_cparams("parallel", "arbitrary"),
        name="diff_attn_sample",
    )(page_table, dq, dk, dv, *lam_params, diff_norm, *([ck] * n_pp), *([cv] * n_pp))


def _merge_kernel(x_ref, ar_ref, ad_ref, gr_ref, gd_ref, wr_ref, wd_ref, wo_ref, o_ref):
    y_ret = jnp.dot(ar_ref[...].astype(BF16), wr_ref[...], preferred_element_type=F32)
    y_diff = jnp.dot(ad_ref[...].astype(BF16), wd_ref[...], preferred_element_type=F32)
    merged = gr_ref[...].astype(F32) * y_ret + gd_ref[...].astype(F32) * y_diff
    o_ref[...] = x_ref[...] + jnp.dot(merged.astype(BF16), wo_ref[...], preferred_element_type=F32)


def _merge(x, a_ret, a_diff, gr, gd, wr, wd, wo):
    n, d = x.shape
    tm = min(MERGE_TM, n)
    row = pl.BlockSpec((tm, d), lambda i: (i, 0))
    return pl.pallas_call(
        _merge_kernel,
        grid=(n // tm,),
        in_specs=[row] * 5 + [_resident(wr.shape), _resident(wd.shape), _resident(wo.shape)],
        out_specs=row,
        out_shape=jax.ShapeDtypeStruct((n, d), F32),
        compiler_params=_cparams("parallel"),
        name="merge",
    )(x, a_ret, a_diff, gr, gd, wr, wd, wo)


def _rotary_tables(pos, dk, hd):
    def cs(half):
        inv = ROPE_THETA ** (-jnp.arange(half, dtype=F32) / half)
        ang = pos[:, None] * inv[None, :]
        return jnp.cos(ang), jnp.sin(ang)

    cr, sr = cs(dk // 2)
    cd, sd = cs(hd // 2)
    zd = jnp.zeros_like(sd)
    reps = LANES // hd
    return (jnp.concatenate([cr, cr], axis=1), jnp.concatenate([-sr, sr], axis=1),
            jnp.tile(cd, (1, 2 * reps)), jnp.tile(jnp.concatenate([-sd, zd], axis=1), (1, reps)),
            jnp.tile(jnp.concatenate([zd, sd], axis=1), (1, reps)))


def _chunk_cols(w, fc):
    d, f = w.shape
    return w.reshape(d, f // fc, fc).transpose(1, 0, 2).astype(BF16)


def kernel(x_prompt, x_sample, cache_k, cache_v, state_ret, page_table, norm_ffn1, ffn1_w1, ffn1_w3, ffn1_w2,
           norm_mix, w_in, ret_norm, w_ret_proj, q_norm, k_norm, lambda_q1, lambda_k1, lambda_q2, lambda_k2,
           diff_norm, w_diff_proj, w_o, norm_ffn2, ffn2_w1, ffn2_w3, ffn2_w2):
    B, S, D = x_prompt.shape
    DB, T, _ = x_sample.shape
    depth = w_in.shape[0]
    hd = D // (2 * N_DIFF_HEADS)
    dk = D // (2 * N_RET_HEADS)
    assert dk == LANES and 2 * hd == LANES, "head layouts assume 128-lane retention heads and 64-lane diff heads"
    assert D % MXU_DIM == 0 and ffn1_w1.shape[-1] % FFN_CHUNK == 0
    past_len = page_table.shape[1] * cache_k.shape[2]

    tabs_p = _rotary_tables(jnp.arange(S, dtype=F32), dk, hd)
    mix_rows_s = min(MIX_TM, DB * T)
    tabs_s = tuple(jnp.tile(t, (mix_rows_s // T, 1))
                   for t in _rotary_tables(past_len + jnp.arange(T, dtype=F32), dk, hd))
    blk = np.arange(MXU_DIM) // hd
    bd = jnp.asarray(blk[:, None] == blk[None, :], dtype=BF16)

    xp = x_prompt.reshape(B * S, D)
    xs = x_sample.reshape(DB * T, D)
    outs = [[] for _ in range(6)]
    for l in range(depth):
        lam_init = 0.8 - 0.6 * math.exp(-0.3 * l)
        row = lambda a: a[l].reshape(1, -1)
        ffn1 = (row(norm_ffn1), _chunk_cols(ffn1_w1[l], FFN_CHUNK), _chunk_cols(ffn1_w3[l], FFN_CHUNK),
                ffn1_w2[l].reshape(-1, FFN_CHUNK, D).astype(BF16))
        ffn2 = (row(norm_ffn2), _chunk_cols(ffn2_w1[l], FFN_CHUNK), _chunk_cols(ffn2_w3[l], FFN_CHUNK),
                ffn2_w2[l].reshape(-1, FFN_CHUNK, D).astype(BF16))
        w_in_b = w_in[l].astype(BF16)
        qg = jnp.tile(row(q_norm), (1, LANES // hd))
        kg = jnp.tile(row(k_norm), (1, LANES // hd))
        lam_params = (row(lambda_q1), row(lambda_k1), row(lambda_q2), row(lambda_k2))
        rn, dn = row(ret_norm), row(diff_norm)
        proj = (w_ret_proj[l].astype(BF16), w_diff_proj[l].astype(BF16), w_o[l].astype(BF16))

        x1 = _ffn_half(xp, *ffn1)
        rq, rk, rv, rg, dq, dk32, dkb, dv32, dvb, gr, gd = _mixer_in(
            x1, row(norm_mix), w_in_b, tabs_p, qg, kg, bd, BF16, hd)
        a_ret, s_p = _retention_prompt(rq, rk, rv, rg, rn, B, S)
        a_diff = _diff_attn_prompt(dq, dkb, dvb, lam_params, dn, B, S, lam_init)
        xp = _ffn_half(_merge(x1, a_ret, a_diff, gr, gd, *proj), *ffn2)
        outs[0].append(dk32.reshape(B, S, N_DIFF_HEADS, 2 * hd))
        outs[1].append(dv32.reshape(B, S, N_DIFF_HEADS, 2 * hd))
        outs[2].append(s_p)

        x1 = _ffn_half(xs, *ffn1)
        rq, rk, rv, rg, dq, dk32, _, dv32, _, gr, gd = _mixer_in(
            x1, row(norm_mix), w_in_b, tabs_s, qg, kg, bd, F32, hd)
        a_ret, s_s = _retention_sample(rq, rk, rv, rg, state_ret[l], rn, T)
        a_diff = _diff_attn_sample(dq, dk32, dv32, cache_k[l], cache_v[l], page_table, lam_params, dn, T,
                                   lam_init)
        xs = _ffn_half(_merge(x1, a_ret, a_diff, gr, gd, *proj), *ffn2)
        outs[3].append(dk32.reshape(DB, T, N_DIFF_HEADS, 2 * hd))
        outs[4].append(dv32.reshape(DB, T, N_DIFF_HEADS, 2 * hd))
        outs[5].append(s_s)

    kp, vp, sp, ks, vs, ss = (jnp.stack(o) for o in outs)
    return xp.reshape(B, S, D), xs.reshape(DB, T, D), kp, vp, sp, ks, vs, ss
```

```python
import functools
import math

import numpy as np
import jax
import jax.numpy as jnp
from jax import lax
from jax.experimental import pallas as pl
from jax.experimental.pallas import tpu as pltpu

F32 = jnp.float32
BF16 = jnp.bfloat16

N_RET_HEADS = 4
N_DIFF_HEADS = 8
RET_CHUNK = 128
ROPE_THETA = 10000.0
NORM_EPS = 1e-6
NEG_INF = -1e30

LANES = 128
MXU_DIM = 256
VMEM_LIMIT = 56 * 1024 * 1024

FFN_TM = 512
FFN_CHUNK = MXU_DIM
MIX_TM = 256
MERGE_TM = 512
FLASH_TQ = 512
PAGES_PER_STEP = 8
LOG2E = math.log2(math.e)
DEC_BLOCK = 16

_NT = (((1,), (1,)), ((), ()))
_TN = (((0,), (0,)), ((), ()))


def _cparams(*sem):
    return pltpu.CompilerParams(dimension_semantics=sem, vmem_limit_bytes=VMEM_LIMIT)


def _resident(shape):
    nd = len(shape)
    return pl.BlockSpec(shape, lambda *_: (0,) * nd, pipeline_mode=pl.Buffered(1))


def _silu(x):
    return x * jax.nn.sigmoid(x)


def _ffn_kernel(x_ref, g_ref, w1_ref, w3_ref, w2_ref, o_ref, h_ref, acc_ref, *, n_chunks):
    x = x_ref[...]
    r = lax.rsqrt(jnp.mean(x * x, axis=-1, keepdims=True) + NORM_EPS)
    h_ref[...] = (x * r * g_ref[...]).astype(BF16)

    for c in range(n_chunks):
        h = h_ref[...]
        a = jnp.dot(h, w1_ref[c], preferred_element_type=F32)
        b = jnp.dot(h, w3_ref[c], preferred_element_type=F32)
        u = (_silu(a) * b).astype(BF16)
        d = jnp.dot(u, w2_ref[c], preferred_element_type=F32)
        if c == 0:
            acc_ref[...] = d
        elif c < n_chunks - 1:
            acc_ref[...] += d
        else:
            o_ref[...] = x_ref[...] + 0.5 * (acc_ref[...] + d)


def _ffn_half(x, g, w1c, w3c, w2c):
    n, d = x.shape
    nc, _, fc = w1c.shape
    tm = min(FFN_TM, n)
    row = pl.BlockSpec((tm, d), lambda i: (i, 0))
    return pl.pallas_call(
        functools.partial(_ffn_kernel, n_chunks=nc),
        grid=(n // tm,),
        in_specs=[row, _resident((1, d)), _resident((nc, d, fc)), _resident((nc, d, fc)),
                  _resident((nc, fc, d))],
        out_specs=row,
        out_shape=jax.ShapeDtypeStruct((n, d), F32),
        scratch_shapes=[pltpu.VMEM((tm, d), BF16), pltpu.VMEM((tm, d), F32)],
        compiler_params=_cparams("parallel"),
        name="ffn_half",
    )(x, g, w1c, w3c, w2c)


def _mix_kernel(x_ref, g_ref, w_ref, cosr_ref, sinr_ref, cosd_ref, sina_ref, sinb_ref, qg_ref, kg_ref,
                bd_ref, rq_o, rk_o, rv_o, rg_o, dq_o, dk_o, dkb_o, dv_o, dvb_o, gr_o, gd_o, *, d, hd):
    x = x_ref[...]
    r = lax.rsqrt(jnp.mean(x * x, axis=-1, keepdims=True) + NORM_EPS)
    h = (x * r * g_ref[...]).astype(BF16)

    def proj(c0, width):
        return jnp.dot(h, w_ref[:, c0:c0 + width], preferred_element_type=F32)

    rw = d // 2
    cosr, sinr = cosr_ref[...], sinr_ref[...]
    for o_ref, c0, scale in ((rq_o, 0, None), (rk_o, rw, LANES ** -0.5)):
        z = proj(c0, rw)
        for hh in range(rw // LANES):
            zh = z[:, hh * LANES:(hh + 1) * LANES]
            y = zh * cosr + pltpu.roll(zh, LANES // 2, 1) * sinr
            if scale is not None:
                y = y * scale
            o_ref[:, hh * LANES:(hh + 1) * LANES] = y.astype(o_ref.dtype)

    rv_o[...] = proj(2 * rw, d).astype(rv_o.dtype)
    rg_o[...] = proj(2 * rw + d, d).astype(rg_o.dtype)

    cosd, sina, sinb = cosd_ref[...], sina_ref[...], sinb_ref[...]
    bd = bd_ref[...]
    c_dq = 2 * rw + 2 * d
    for c0, gain_ref, is_q in ((c_dq, qg_ref, True), (c_dq + d, kg_ref, False)):
        z = proj(c0, d)
        gain = gain_ref[...]
        for cc in range(d // MXU_DIM):
            zc = z[:, cc * MXU_DIM:(cc + 1) * MXU_DIM]
            ss = jnp.dot((zc * zc).astype(BF16), bd, preferred_element_type=F32)
            yn = zc * lax.rsqrt(ss * (1.0 / hd) + NORM_EPS)
            for s2 in range(MXU_DIM // LANES):
                y = yn[:, s2 * LANES:(s2 + 1) * LANES] * gain
                y = (y * cosd + pltpu.roll(y, LANES - hd // 2, 1) * sina
                     + pltpu.roll(y, hd // 2, 1) * sinb)
                col = cc * MXU_DIM + s2 * LANES
                if is_q:
                    dq_o[:, col:col + LANES] = (y * (hd ** -0.5 * LOG2E)).astype(dq_o.dtype)
                else:
                    dk_o[:, col:col + LANES] = y
                    dkb_o[:, col:col + LANES] = y.astype(dkb_o.dtype)

    z = proj(c_dq + 2 * d, d)
    dv_o[...] = z
    dvb_o[...] = z.astype(dvb_o.dtype)
    gr_o[...] = jax.nn.sigmoid(proj(c_dq + 3 * d, d)).astype(gr_o.dtype)
    gd_o[...] = jax.nn.sigmoid(proj(c_dq + 4 * d, d)).astype(gd_o.dtype)


def _mixer_in(x, g, w_in, tabs, qg, kg, bd, act_dtype, hd):
    n, d = x.shape
    tm = min(MIX_TM, n)
    n_tab = tabs[0].shape[0] // tm
    row = lambda w: pl.BlockSpec((tm, w), lambda i: (i, 0))
    tab = pl.BlockSpec((tm, LANES), lambda i: (i % n_tab, 0))
    sds = lambda w, dt: jax.ShapeDtypeStruct((n, w), dt)
    widths = (d // 2, d // 2, d, d, d, d, d, d, d, d, d)
    dtypes = (act_dtype, act_dtype, act_dtype, act_dtype, act_dtype, F32, act_dtype, F32, act_dtype,
              act_dtype, act_dtype)
    return pl.pallas_call(
        functools.partial(_mix_kernel, d=d, hd=hd),
        grid=(n // tm,),
        in_specs=[row(d), _resident((1, d)), _resident(w_in.shape), tab, tab, tab, tab, tab,
                  _resident((1, LANES)), _resident((1, LANES)), _resident((MXU_DIM, MXU_DIM))],
        out_specs=[row(w) for w in widths],
        out_shape=[sds(w, dt) for w, dt in zip(widths, dtypes)],
        compiler_params=_cparams("parallel"),
        name="mixer_in",
    )(x, g, w_in, *tabs, qg, kg, bd)


def _ret_out(o, gate, norm_g):
    r = lax.rsqrt(jnp.mean(o * o, axis=-1, keepdims=True) + NORM_EPS)
    return _silu(gate) * (o * r * norm_g)


def _ret_kernel(q_ref, k_ref, v_ref, g_ref, dm_ref, qd_ref, kd_ref, rn_ref, a_ref, s_ref, *, cdec, dk, dv):
    @pl.when(pl.program_id(1) == 0)
    def _():
        s_ref[...] = jnp.zeros_like(s_ref)

    for h in range(N_RET_HEADS):
        ks = slice(h * dk, (h + 1) * dk)
        vs = slice(h * dv, (h + 1) * dv)
        q, k, v = q_ref[:, ks], k_ref[:, ks], v_ref[:, vs]
        att = lax.dot_general(q, k, _NT, preferred_element_type=F32) * dm_ref[h]
        inner = jnp.dot(att.astype(BF16), v, preferred_element_type=F32)
        s = s_ref[0, h]
        qdec = (q.astype(F32) * qd_ref[:, ks]).astype(BF16)
        cross = jnp.dot(qdec, s.astype(BF16), preferred_element_type=F32)
        kdec = (k.astype(F32) * kd_ref[:, ks]).astype(BF16)
        s_ref[0, h] = s * cdec[h] + lax.dot_general(kdec, v, _TN, preferred_element_type=F32)
        a_ref[:, vs] = _ret_out(inner + cross, g_ref[:, vs].astype(F32), rn_ref[:, vs]).astype(a_ref.dtype)


def _ret_tables(chunk, dk):
    h = np.arange(N_RET_HEADS, dtype=np.float64)
    lg = np.log1p(-(2.0 ** (-5.0 - h)))
    idx = np.arange(chunk, dtype=np.float64)
    diff = idx[:, None] - idx[None, :]
    dmask = np.where(diff >= 0, np.exp(np.maximum(diff, 0.0)[None] * lg[:, None, None]), 0.0)
    q_dec = np.exp((idx[:, None] + 1.0) * lg[None, :])
    k_dec = np.exp((chunk - 1.0 - idx)[:, None] * lg[None, :])
    c_dec = tuple(float(c) for c in np.exp(chunk * lg))
    rep = lambda t: np.repeat(t, dk, axis=1)
    return dmask.astype(np.float32), rep(q_dec).astype(np.float32), rep(k_dec).astype(np.float32), c_dec


def _retention_prompt(rq, rk, rv, rg, ret_norm, batch, seq):
    n, hk = rq.shape
    hv = rv.shape[1]
    dk, dv = hk // N_RET_HEADS, hv // N_RET_HEADS
    chunk = RET_CHUNK if seq % RET_CHUNK == 0 else seq
    nc = seq // chunk
    dmask, q_dec, k_dec, c_dec = _ret_tables(chunk, dk)
    row = lambda w: pl.BlockSpec((chunk, w), lambda b, c: (b * nc + c, 0))
    return pl.pallas_call(
        functools.partial(_ret_kernel, cdec=c_dec, dk=dk, dv=dv),
        grid=(batch, nc),
        in_specs=[row(hk), row(hk), row(hv), row(hv), _resident(dmask.shape), _resident(q_dec.shape),
                  _resident(k_dec.shape), _resident((1, hv))],
        out_specs=[row(hv), pl.BlockSpec((1, N_RET_HEADS, dk, dv), lambda b, c: (b, 0, 0, 0))],
        out_shape=[jax.ShapeDtypeStruct((n, hv), BF16),
                   jax.ShapeDtypeStruct((batch, N_RET_HEADS, dk, dv), F32)],
        compiler_params=_cparams("parallel", "arbitrary"),
        name="retention_prompt",
    )(rq, rk, rv, rg, jnp.asarray(dmask), jnp.asarray(q_dec), jnp.asarray(k_dec), ret_norm)


def _ret_dec_kernel(q_ref, k_ref, v_ref, g_ref, s_ref, dm_ref, qd_ref, kd_ref, rn_ref, a_ref, so_ref,
                    cross_ref, *, cdec, dk, dv, t_len):
    rows = q_ref.shape[0]
    nb = rows // t_len
    row_seq = lax.broadcasted_iota(jnp.int32, (rows, dk), 0) // t_len

    def body(b, carry):
        r0 = pl.multiple_of(b * t_len, t_len)
        for h in range(N_RET_HEADS):
            ks = slice(h * dk, (h + 1) * dk)
            vs = slice(h * dv, (h + 1) * dv)
            s = s_ref[b, h]
            qb = q_ref[pl.ds(r0, t_len), ks] * qd_ref[pl.ds(r0, t_len), ks]
            cross_ref[pl.ds(r0, t_len), vs] = jnp.dot(qb, s, preferred_element_type=F32)
            kb = jnp.where(row_seq == b, k_ref[:, ks] * kd_ref[:, ks], 0.0)
            so_ref[b, h] = s * cdec[h] + lax.dot_general(kb, v_ref[:, vs], _TN, preferred_element_type=F32)
        return carry

    lax.fori_loop(0, nb, body, 0)

    for h in range(N_RET_HEADS):
        ks = slice(h * dk, (h + 1) * dk)
        vs = slice(h * dv, (h + 1) * dv)
        att = lax.dot_general(q_ref[:, ks], k_ref[:, ks], _NT, preferred_element_type=F32) * dm_ref[h]
        inner = jnp.dot(att, v_ref[:, vs], preferred_element_type=F32)
        a_ref[:, vs] = _ret_out(inner + cross_ref[:, vs], g_ref[:, vs], rn_ref[:, vs]).astype(a_ref.dtype)


def _retention_sample(rq, rk, rv, rg, state, ret_norm, t_len, layer):
    n, hk = rq.shape
    hv = rv.shape[1]
    dk, dv = hk // N_RET_HEADS, hv // N_RET_HEADS
    nseq = n // t_len
    bb = min(DEC_BLOCK, nseq)
    rows = bb * t_len
    dmask, q_dec, k_dec, c_dec = _ret_tables(t_len, dk)
    same_seq = np.kron(np.eye(bb, dtype=np.float32), np.ones((t_len, t_len), np.float32))
    dmask = np.tile(dmask, (1, bb, bb)) * same_seq[None]
    q_dec, k_dec = np.tile(q_dec, (bb, 1)), np.tile(k_dec, (bb, 1))
    row = lambda w: pl.BlockSpec((rows, w), lambda i: (i, 0))
    nblk = nseq // bb
    st_shape = (bb, N_RET_HEADS, dk, dv)
    st_in = pl.BlockSpec(st_shape, lambda i: (layer * nblk + i, 0, 0, 0))
    st_out = pl.BlockSpec(st_shape, lambda i: (i, 0, 0, 0))
    state = state.reshape((-1,) + state.shape[2:])
    return pl.pallas_call(
        functools.partial(_ret_dec_kernel, cdec=c_dec, dk=dk, dv=dv, t_len=t_len),
        grid=(nblk,),
        in_specs=[row(hk), row(hk), row(hv), row(hv), st_in, _resident(dmask.shape), _resident(q_dec.shape),
                  _resident(k_dec.shape), _resident((1, hv))],
        out_specs=[row(hv), st_out],
        out_shape=[jax.ShapeDtypeStruct((n, hv), F32),
                   jax.ShapeDtypeStruct((nseq, N_RET_HEADS, dk, dv), F32)],
        scratch_shapes=[pltpu.VMEM((rows, hv), F32)],
        compiler_params=_cparams("parallel"),
        name="retention_sample",
    )(rq, rk, rv, rg, state, jnp.asarray(dmask), jnp.asarray(q_dec), jnp.asarray(k_dec), ret_norm)


def _lambda(lq1_ref, lk1_ref, lq2_ref, lk2_ref, lam_init):
    s1 = jnp.sum(lq1_ref[...] * lk1_ref[...], axis=-1, keepdims=True)
    s2 = jnp.sum(lq2_ref[...] * lk2_ref[...], axis=-1, keepdims=True)
    return jnp.exp(s1) - jnp.exp(s2) + lam_init


def _diff_out(o, norm_g, lam_init):
    r = lax.rsqrt(jnp.mean(o * o, axis=-1, keepdims=True) + NORM_EPS)
    return o * r * norm_g * (1.0 - lam_init)


def _flash_kernel(q_ref, k_ref, v_ref, lq1_ref, lk1_ref, lq2_ref, lk2_ref, dn_ref, o_ref, qs_ref, vx_ref,
                  s_ref, mx_ref, mb_ref, acc_ref, *, tq, hd, lam_init):
    i = pl.program_id(2)
    dv = v_ref.shape[1]
    n_lane = tq // LANES
    halves = (slice(0, tq), slice(tq, 2 * tq))

    @pl.when(i == 0)
    def _():
        vx_ref[:, 0:dv] = v_ref[...]
        vx_ref[:, dv:2 * dv] = jnp.ones(v_ref.shape, BF16)

    lam = _lambda(lq1_ref, lk1_ref, lq2_ref, lk2_ref, lam_init)
    q = q_ref[...]
    lane = lax.broadcasted_iota(jnp.int32, q.shape, 1)
    zero = jnp.zeros_like(q)
    qs_ref[0:tq, :] = jnp.where(lane < hd, q, zero)
    qs_ref[tq:2 * tq, :] = jnp.where(lane >= hd, q, zero)
    mx_ref[...] = jnp.full_like(mx_ref, NEG_INF)
    acc_ref[...] = jnp.zeros_like(acc_ref)

    def score_block(j, masked):
        kb = k_ref[pl.ds(pl.multiple_of(j * tq, tq), tq), :]
        for rows in halves:
            s = lax.dot_general(qs_ref[rows, :], kb, _NT, preferred_element_type=F32)
            if masked:
                tri = (lax.broadcasted_iota(jnp.int32, s.shape, 1)
                       <= lax.broadcasted_iota(jnp.int32, s.shape, 0))
                s = jnp.where(tri, s, NEG_INF)
            s_ref[j, rows, :] = s
            part = s[:, 0:LANES]
            for c in range(1, n_lane):
                part = jnp.maximum(part, s[:, c * LANES:(c + 1) * LANES])
            mx_ref[rows, :] = jnp.maximum(mx_ref[rows, :], part)

    def score_pair(t, carry):
        score_block(2 * t, False)
        score_block(2 * t + 1, False)
        return carry

    lax.fori_loop(0, i // 2, score_pair, 0)

    @pl.when(i % 2 == 1)
    def _():
        score_block(i - 1, False)
        score_block(i, True)

    @pl.when(i % 2 == 0)
    def _():
        score_block(i, True)

    mb_ref[...] = jnp.broadcast_to(jnp.max(mx_ref[...], axis=-1, keepdims=True), mb_ref.shape)

    def pv_blocks(js):
        for rows in halves:
            mb = mb_ref[rows, :]
            tot = None
            for j in js:
                p = jnp.concatenate(
                    [jnp.exp2(s_ref[j, rows, c * LANES:(c + 1) * LANES] - mb) for c in range(n_lane)],
                    axis=1).astype(BF16)
                d = jnp.dot(p, vx_ref[pl.ds(pl.multiple_of(j * tq, tq), tq), :], preferred_element_type=F32)
                tot = d if tot is None else tot + d
            acc_ref[rows, :] += tot

    def pv_pair(t, carry):
        pv_blocks((2 * t, 2 * t + 1))
        return carry

    lax.fori_loop(0, (i + 1) // 2, pv_pair, 0)

    @pl.when(i % 2 == 0)
    def _():
        pv_blocks((i,))

    acc = acc_ref[...]
    o = acc[:, 0:dv] / acc[:, dv:2 * dv]
    o_ref[...] = _diff_out(o[0:tq] - lam * o[tq:2 * tq], dn_ref[...], lam_init).astype(o_ref.dtype)


def _diff_attn_prompt(dq, dk, dv, lam_params, diff_norm, batch, seq, lam_init):
    n, hw = dq.shape
    hd = hw // (2 * N_DIFF_HEADS)
    w = 2 * hd
    tq = min(FLASH_TQ, seq)
    nq = seq // tq
    assert seq % tq == 0 and tq % LANES == 0
    qrow = pl.BlockSpec((tq, w), lambda b, h, i: (b * nq + i, h))
    kv = pl.BlockSpec((seq, w), lambda b, h, i: (b, h))
    return pl.pallas_call(
        functools.partial(_flash_kernel, tq=tq, hd=hd, lam_init=lam_init),
        grid=(batch, N_DIFF_HEADS, nq),
        in_specs=[qrow, kv, kv] + [_resident((1, hd))] * 4 + [_resident((1, w))],
        out_specs=qrow,
        out_shape=jax.ShapeDtypeStruct((n, hw), BF16),
        scratch_shapes=[pltpu.VMEM((2 * tq, w), BF16), pltpu.VMEM((seq, 2 * w), BF16),
                        pltpu.VMEM((nq, 2 * tq, tq), F32), pltpu.VMEM((2 * tq, LANES), F32),
                        pltpu.VMEM((2 * tq, LANES), F32), pltpu.VMEM((2 * tq, 2 * w), F32)],
        compiler_params=_cparams("parallel", "parallel", "arbitrary"),
        name="diff_attn_prompt",
    )(dq, dk, dv, *lam_params, diff_norm)


def _paged_kernel(pt_ref, q_ref, kn_ref, vn_ref, lq1_ref, lk1_ref, lq2_ref, lk2_ref, dn_ref, *rest,
                  n_pp, t_len, hd, lam_init):
    k_refs, v_refs = rest[:n_pp], rest[n_pp:2 * n_pp]
    o_ref, qbd_ref, m_ref, l_ref, acc_ref = rest[2 * n_pp:]
    p = pl.program_id(1)
    n_rows, width = qbd_ref.shape

    @pl.when(p == 0)
    def _():
        q = jnp.tile(q_ref[...], (n_rows // t_len, 1))
        r = lax.broadcasted_iota(jnp.int32, q.shape, 0) // t_len
        c = lax.broadcasted_iota(jnp.int32, q.shape, 1) // hd
        qbd_ref[...] = jnp.where(r == c, q, 0.0).astype(BF16)
        m_ref[...] = jnp.full_like(m_ref, NEG_INF)
        l_ref[...] = jnp.zeros_like(l_ref)
        acc_ref[...] = jnp.zeros_like(acc_ref)

    qbd = qbd_ref[...]

    def update(s, v):
        m_prev = m_ref[...]
        m_new = jnp.maximum(m_prev, jnp.max(s, axis=-1, keepdims=True))
        corr = jnp.exp2(m_prev - m_new)
        pe = jnp.exp2(s - m_new)
        l_ref[...] = l_ref[...] * corr + jnp.sum(pe, axis=-1, keepdims=True)
        acc_ref[...] = acc_ref[...] * corr + jnp.dot(pe.astype(BF16), v, preferred_element_type=F32)
        m_ref[...] = m_new

    k_all = jnp.concatenate([k_refs[r][...].astype(BF16) for r in range(n_pp)], axis=0)
    v_all = jnp.concatenate([v_refs[r][...].astype(BF16) for r in range(n_pp)], axis=0)
    update(lax.dot_general(qbd, k_all, _NT, preferred_element_type=F32), v_all)

    @pl.when(p == pl.num_programs(1) - 1)
    def _():
        pad = jnp.zeros((LANES - t_len, width), F32)
        kn = jnp.concatenate([kn_ref[...], pad], axis=0).astype(BF16)
        vn = jnp.concatenate([vn_ref[...], pad], axis=0).astype(BF16)
        sn = lax.dot_general(qbd, kn, _NT, preferred_element_type=F32)
        qpos = lax.broadcasted_iota(jnp.int32, sn.shape, 0) % t_len
        kpos = lax.broadcasted_iota(jnp.int32, sn.shape, 1)
        sn = jnp.where(kpos <= qpos, sn, NEG_INF)
        update(sn, vn)
        o = acc_ref[...] / l_ref[...]
        lam = _lambda(lq1_ref, lk1_ref, lq2_ref, lk2_ref, lam_init)
        dvw = width // N_DIFF_HEADS
        for h in range(N_DIFF_HEADS):
            cs = slice(h * dvw, (h + 1) * dvw)
            o1 = o[(2 * h) * t_len:(2 * h + 1) * t_len, cs]
            o2 = o[(2 * h + 1) * t_len:(2 * h + 2) * t_len, cs]
            o_ref[:, cs] = _diff_out(o1 - lam * o2, dn_ref[...], lam_init).astype(o_ref.dtype)


def _diff_attn_sample(dq, dk, dv, cache_k, cache_v, page_table, lam_params, diff_norm, t_len, lam_init, layer):
    n, hw = dq.shape
    hd = hw // (2 * N_DIFF_HEADS)
    nseq, n_pages = page_table.shape
    depth, n_pool, page = cache_k.shape[:3]
    ck = cache_k.reshape(depth * n_pool, page, hw)
    cv = cache_v.reshape(depth * n_pool, page, hw)
    page_table = page_table + layer * n_pool
    n_pp = math.gcd(PAGES_PER_STEP, n_pages)
    n_rows = 2 * N_DIFF_HEADS * t_len
    row = pl.BlockSpec((t_len, hw), lambda b, p, pt: (b, 0))
    const = lambda shape: pl.BlockSpec(shape, lambda b, p, pt: (0, 0))
    page_spec = lambda r: pl.BlockSpec((None, page, hw), lambda b, p, pt: (pt[b, p * n_pp + r], 0, 0))
    grid_spec = pltpu.PrefetchScalarGridSpec(
        num_scalar_prefetch=1,
        grid=(nseq, n_pages // n_pp),
        in_specs=[row, row, row] + [const((1, hd))] * 4 + [const((1, 2 * hd))]
        + [page_spec(r) for r in range(n_pp)] * 2,
        out_specs=row,
        scratch_shapes=[pltpu.VMEM((n_rows, hw), BF16), pltpu.VMEM((n_rows, 1), F32),
                        pltpu.VMEM((n_rows, 1), F32), pltpu.VMEM((n_rows, hw), F32)],
    )
    return pl.pallas_call(
        functools.partial(_paged_kernel, n_pp=n_pp, t_len=t_len, hd=hd, lam_init=lam_init),
        grid_spec=grid_spec,
        out_shape=jax.ShapeDtypeStruct((n, hw), F32),
        compiler_params=_cparams("parallel", "arbitrary"),
        name="diff_attn_sample",
    )(page_table, dq, dk, dv, *lam_params, diff_norm, *([ck] * n_pp), *([cv] * n_pp))


def _merge_kernel(x_ref, ar_ref, ad_ref, gr_ref, gd_ref, wr_ref, wd_ref, wo_ref, o_ref):
    y_ret = jnp.dot(ar_ref[...].astype(BF16), wr_ref[...], preferred_element_type=F32)
    y_diff = jnp.dot(ad_ref[...].astype(BF16), wd_ref[...], preferred_element_type=F32)
    merged = gr_ref[...].astype(F32) * y_ret + gd_ref[...].astype(F32) * y_diff
    o_ref[...] = x_ref[...] + jnp.dot(merged.astype(BF16), wo_ref[...], preferred_element_type=F32)


def _merge(x, a_ret, a_diff, gr, gd, wr, wd, wo):
    n, d = x.shape
    tm = min(MERGE_TM, n)
    row = pl.BlockSpec((tm, d), lambda i: (i, 0))
    return pl.pallas_call(
        _merge_kernel,
        grid=(n // tm,),
        in_specs=[row] * 5 + [_resident(wr.shape), _resident(wd.shape), _resident(wo.shape)],
        out_specs=row,
        out_shape=jax.ShapeDtypeStruct((n, d), F32),
        compiler_params=_cparams("parallel"),
        name="merge",
    )(x, a_ret, a_diff, gr, gd, wr, wd, wo)


def _rotary_tables(pos, dk, hd):
    def cs(half):
        inv = ROPE_THETA ** (-jnp.arange(half, dtype=F32) / half)
        ang = pos[:, None] * inv[None, :]
        return jnp.cos(ang), jnp.sin(ang)

    cr, sr = cs(dk // 2)
    cd, sd = cs(hd // 2)
    zd = jnp.zeros_like(sd)
    reps = LANES // hd
    return (jnp.concatenate([cr, cr], axis=1), jnp.concatenate([-sr, sr], axis=1),
            jnp.tile(cd, (1, 2 * reps)), jnp.tile(jnp.concatenate([-sd, zd], axis=1), (1, reps)),
            jnp.tile(jnp.concatenate([zd, sd], axis=1), (1, reps)))


def _chunk_cols(w, fc):
    d, f = w.shape
    return w.reshape(d, f // fc, fc).transpose(1, 0, 2).astype(BF16)


def kernel(x_prompt, x_sample, cache_k, cache_v, state_ret, page_table, norm_ffn1, ffn1_w1, ffn1_w3, ffn1_w2,
           norm_mix, w_in, ret_norm, w_ret_proj, q_norm, k_norm, lambda_q1, lambda_k1, lambda_q2, lambda_k2,
           diff_norm, w_diff_proj, w_o, norm_ffn2, ffn2_w1, ffn2_w3, ffn2_w2):
    B, S, D = x_prompt.shape
    DB, T, _ = x_sample.shape
    depth = w_in.shape[0]
    hd = D // (2 * N_DIFF_HEADS)
    dk = D // (2 * N_RET_HEADS)
    assert dk == LANES and 2 * hd == LANES, "head layouts assume 128-lane retention heads and 64-lane diff heads"
    assert D % MXU_DIM == 0 and ffn1_w1.shape[-1] % FFN_CHUNK == 0
    past_len = page_table.shape[1] * cache_k.shape[2]

    tabs_p = _rotary_tables(jnp.arange(S, dtype=F32), dk, hd)
    mix_rows_s = min(MIX_TM, DB * T)
    tabs_s = tuple(jnp.tile(t, (mix_rows_s // T, 1))
                   for t in _rotary_tables(past_len + jnp.arange(T, dtype=F32), dk, hd))
    blk = np.arange(MXU_DIM) // hd
    bd = jnp.asarray(blk[:, None] == blk[None, :], dtype=BF16)

    xp = x_prompt.reshape(B * S, D)
    xs = x_sample.reshape(DB * T, D)
    outs = [[] for _ in range(6)]
    for l in range(depth):
        lam_init = 0.8 - 0.6 * math.exp(-0.3 * l)
        row = lambda a: a[l].reshape(1, -1)
        ffn1 = (row(norm_ffn1), _chunk_cols(ffn1_w1[l], FFN_CHUNK), _chunk_cols(ffn1_w3[l], FFN_CHUNK),
                ffn1_w2[l].reshape(-1, FFN_CHUNK, D).astype(BF16))
        ffn2 = (row(norm_ffn2), _chunk_cols(ffn2_w1[l], FFN_CHUNK), _chunk_cols(ffn2_w3[l], FFN_CHUNK),
                ffn2_w2[l].reshape(-1, FFN_CHUNK, D).astype(BF16))
        w_in_b = w_in[l].astype(BF16)
        qg = jnp.tile(row(q_norm), (1, LANES // hd))
        kg = jnp.tile(row(k_norm), (1, LANES // hd))
        lam_params = (row(lambda_q1), row(lambda_k1), row(lambda_q2), row(lambda_k2))
        rn, dn = row(ret_norm), row(diff_norm)
        proj = (w_ret_proj[l].astype(BF16), w_diff_proj[l].astype(BF16), w_o[l].astype(BF16))

        x1 = _ffn_half(xp, *ffn1)
        rq, rk, rv, rg, dq, dk32, dkb, dv32, dvb, gr, gd = _mixer_in(
            x1, row(norm_mix), w_in_b, tabs_p, qg, kg, bd, BF16, hd)
        a_ret, s_p = _retention_prompt(rq, rk, rv, rg, rn, B, S)
        a_diff = _diff_attn_prompt(dq, dkb, dvb, lam_params, dn, B, S, lam_init)
        xp = _ffn_half(_merge(x1, a_ret, a_diff, gr, gd, *proj), *ffn2)
        outs[0].append(dk32.reshape(B, S, N_DIFF_HEADS, 2 * hd))
        outs[1].append(dv32.reshape(B, S, N_DIFF_HEADS, 2 * hd))
        outs[2].append(s_p)

        x1 = _ffn_half(xs, *ffn1)
        rq, rk, rv, rg, dq, dk32, _, dv32, _, gr, gd = _mixer_in(
            x1, row(norm_mix), w_in_b, tabs_s, qg, kg, bd, F32, hd)
        a_ret, s_s = _retention_sample(rq, rk, rv, rg, state_ret, rn, T, l)
        a_diff = _diff_attn_sample(dq, dk32, dv32, cache_k, cache_v, page_table, lam_params, dn, T,
                                   lam_init, l)
        xs = _ffn_half(_merge(x1, a_ret, a_diff, gr, gd, *proj), *ffn2)
        outs[3].append(dk32.reshape(DB, T, N_DIFF_HEADS, 2 * hd))
        outs[4].append(dv32.reshape(DB, T, N_DIFF_HEADS, 2 * hd))
        outs[5].append(s_s)

    kp, vp, sp, ks, vs, ss = (jnp.stack(o) for o in outs)
    return xp.reshape(B, S, D), xs.reshape(DB, T, D), kp, vp, sp, ks, vs, ss
```

```python
import functools
import math

import numpy as np
import jax
import jax.numpy as jnp
from jax import lax
from jax.experimental import pallas as pl
from jax.experimental.pallas import tpu as pltpu

F32 = jnp.float32
BF16 = jnp.bfloat16

N_RET_HEADS = 4
N_DIFF_HEADS = 8
RET_CHUNK = 128
ROPE_THETA = 10000.0
NORM_EPS = 1e-6
NEG_INF = -1e30

LANES = 128
MXU_DIM = 256
VMEM_LIMIT = 56 * 1024 * 1024

FFN_TM = 512
FFN_CHUNK = MXU_DIM
MIX_TM = 256
MERGE_TM = 512
FLASH_TQ = 512
PAGES_PER_STEP = 8
LOG2E = math.log2(math.e)
DEC_BLOCK = 16

_NT = (((1,), (1,)), ((), ()))
_TN = (((0,), (0,)), ((), ()))


def _cparams(*sem):
    return pltpu.CompilerParams(dimension_semantics=sem, vmem_limit_bytes=VMEM_LIMIT)


def _resident(shape):
    nd = len(shape)
    return pl.BlockSpec(shape, lambda *_: (0,) * nd, pipeline_mode=pl.Buffered(1))


def _silu(x):
    return x * jax.nn.sigmoid(x)


def _ffn_kernel(x_ref, g_ref, w1_ref, w3_ref, w2_ref, o_ref, h_ref, acc_ref, *, n_chunks):
    x = x_ref[...]
    r = lax.rsqrt(jnp.mean(x * x, axis=-1, keepdims=True) + NORM_EPS)
    h_ref[...] = (x * r * g_ref[...]).astype(BF16)

    for c in range(n_chunks):
        h = h_ref[...]
        a = jnp.dot(h, w1_ref[c], preferred_element_type=F32)
        b = jnp.dot(h, w3_ref[c], preferred_element_type=F32)
        u = (_silu(a) * b).astype(BF16)
        d = jnp.dot(u, w2_ref[c], preferred_element_type=F32)
        if c == 0:
            acc_ref[...] = d
        elif c < n_chunks - 1:
            acc_ref[...] += d
        else:
            o_ref[...] = x_ref[...] + 0.5 * (acc_ref[...] + d)


def _ffn_half(x, g, w1c, w3c, w2c):
    n, d = x.shape
    nc, _, fc = w1c.shape
    tm = min(FFN_TM, n)
    row = pl.BlockSpec((tm, d), lambda i: (i, 0))
    return pl.pallas_call(
        functools.partial(_ffn_kernel, n_chunks=nc),
        grid=(n // tm,),
        in_specs=[row, _resident((1, d)), _resident((nc, d, fc)), _resident((nc, d, fc)),
                  _resident((nc, fc, d))],
        out_specs=row,
        out_shape=jax.ShapeDtypeStruct((n, d), F32),
        scratch_shapes=[pltpu.VMEM((tm, d), BF16), pltpu.VMEM((tm, d), F32)],
        compiler_params=_cparams("parallel"),
        name="ffn_half",
    )(x, g, w1c, w3c, w2c)


def _mix_kernel(x_ref, g_ref, w_ref, cosr_ref, sinr_ref, cosd_ref, sina_ref, sinb_ref, qg_ref, kg_ref,
                bd_ref, rq_o, rk_o, rv_o, rg_o, dq_o, dk_o, dkb_o, dv_o, dvb_o, gr_o, gd_o, *, d, hd):
    x = x_ref[...]
    r = lax.rsqrt(jnp.mean(x * x, axis=-1, keepdims=True) + NORM_EPS)
    h = (x * r * g_ref[...]).astype(BF16)

    def proj(c0, width):
        return jnp.dot(h, w_ref[:, c0:c0 + width], preferred_element_type=F32)

    rw = d // 2
    cosr, sinr = cosr_ref[...], sinr_ref[...]
    for o_ref, c0, scale in ((rq_o, 0, None), (rk_o, rw, LANES ** -0.5)):
        z = proj(c0, rw)
        for hh in range(rw // LANES):
            zh = z[:, hh * LANES:(hh + 1) * LANES]
            y = zh * cosr + pltpu.roll(zh, LANES // 2, 1) * sinr
            if scale is not None:
                y = y * scale
            o_ref[:, hh * LANES:(hh + 1) * LANES] = y.astype(o_ref.dtype)

    rv_o[...] = proj(2 * rw, d).astype(rv_o.dtype)
    rg_o[...] = proj(2 * rw + d, d).astype(rg_o.dtype)

    cosd, sina, sinb = cosd_ref[...], sina_ref[...], sinb_ref[...]
    bd = bd_ref[...]
    c_dq = 2 * rw + 2 * d
    for c0, gain_ref, is_q in ((c_dq, qg_ref, True), (c_dq + d, kg_ref, False)):
        z = proj(c0, d)
        gain = gain_ref[...]
        for cc in range(d // MXU_DIM):
            zc = z[:, cc * MXU_DIM:(cc + 1) * MXU_DIM]
            ss = jnp.dot((zc * zc).astype(BF16), bd, preferred_element_type=F32)
            yn = zc * lax.rsqrt(ss * (1.0 / hd) + NORM_EPS)
            for s2 in range(MXU_DIM // LANES):
                y = yn[:, s2 * LANES:(s2 + 1) * LANES] * gain
                y = (y * cosd + pltpu.roll(y, LANES - hd // 2, 1) * sina
                     + pltpu.roll(y, hd // 2, 1) * sinb)
                col = cc * MXU_DIM + s2 * LANES
                if is_q:
                    dq_o[:, col:col + LANES] = (y * (hd ** -0.5 * LOG2E)).astype(dq_o.dtype)
                else:
                    dk_o[:, col:col + LANES] = y
                    dkb_o[:, col:col + LANES] = y.astype(dkb_o.dtype)

    z = proj(c_dq + 2 * d, d)
    dv_o[...] = z
    dvb_o[...] = z.astype(dvb_o.dtype)
    gr_o[...] = jax.nn.sigmoid(proj(c_dq + 3 * d, d)).astype(gr_o.dtype)
    gd_o[...] = jax.nn.sigmoid(proj(c_dq + 4 * d, d)).astype(gd_o.dtype)


def _mixer_in(x, g, w_in, tabs, qg, kg, bd, act_dtype, hd):
    n, d = x.shape
    tm = min(MIX_TM, n)
    n_tab = tabs[0].shape[0] // tm
    row = lambda w: pl.BlockSpec((tm, w), lambda i: (i, 0))
    tab = pl.BlockSpec((tm, LANES), lambda i: (i % n_tab, 0))
    sds = lambda w, dt: jax.ShapeDtypeStruct((n, w), dt)
    widths = (d // 2, d // 2, d, d, d, d, d, d, d, d, d)
    dtypes = (act_dtype, act_dtype, act_dtype, act_dtype, act_dtype, F32, act_dtype, F32, act_dtype,
              act_dtype, act_dtype)
    return pl.pallas_call(
        functools.partial(_mix_kernel, d=d, hd=hd),
        grid=(n // tm,),
        in_specs=[row(d), _resident((1, d)), _resident(w_in.shape), tab, tab, tab, tab, tab,
                  _resident((1, LANES)), _resident((1, LANES)), _resident((MXU_DIM, MXU_DIM))],
        out_specs=[row(w) for w in widths],
        out_shape=[sds(w, dt) for w, dt in zip(widths, dtypes)],
        compiler_params=_cparams("parallel"),
        name="mixer_in",
    )(x, g, w_in, *tabs, qg, kg, bd)


def _ret_out(o, gate, norm_g):
    r = lax.rsqrt(jnp.mean(o * o, axis=-1, keepdims=True) + NORM_EPS)
    return _silu(gate) * (o * r * norm_g)


def _ret_kernel(q_ref, k_ref, v_ref, g_ref, dm_ref, qd_ref, kd_ref, rn_ref, a_ref, s_ref, *, cdec, dk, dv):
    @pl.when(pl.program_id(1) == 0)
    def _():
        s_ref[...] = jnp.zeros_like(s_ref)

    for h in range(N_RET_HEADS):
        ks = slice(h * dk, (h + 1) * dk)
        vs = slice(h * dv, (h + 1) * dv)
        q, k, v = q_ref[:, ks], k_ref[:, ks], v_ref[:, vs]
        att = lax.dot_general(q, k, _NT, preferred_element_type=F32) * dm_ref[h]
        inner = jnp.dot(att.astype(BF16), v, preferred_element_type=F32)
        s = s_ref[0, h]
        qdec = (q.astype(F32) * qd_ref[:, ks]).astype(BF16)
        cross = jnp.dot(qdec, s.astype(BF16), preferred_element_type=F32)
        kdec = (k.astype(F32) * kd_ref[:, ks]).astype(BF16)
        s_ref[0, h] = s * cdec[h] + lax.dot_general(kdec, v, _TN, preferred_element_type=F32)
        a_ref[:, vs] = _ret_out(inner + cross, g_ref[:, vs].astype(F32), rn_ref[:, vs]).astype(a_ref.dtype)


def _ret_tables(chunk, dk):
    h = np.arange(N_RET_HEADS, dtype=np.float64)
    lg = np.log1p(-(2.0 ** (-5.0 - h)))
    idx = np.arange(chunk, dtype=np.float64)
    diff = idx[:, None] - idx[None, :]
    dmask = np.where(diff >= 0, np.exp(np.maximum(diff, 0.0)[None] * lg[:, None, None]), 0.0)
    q_dec = np.exp((idx[:, None] + 1.0) * lg[None, :])
    k_dec = np.exp((chunk - 1.0 - idx)[:, None] * lg[None, :])
    c_dec = tuple(float(c) for c in np.exp(chunk * lg))
    rep = lambda t: np.repeat(t, dk, axis=1)
    return dmask.astype(np.float32), rep(q_dec).astype(np.float32), rep(k_dec).astype(np.float32), c_dec


def _retention_prompt(rq, rk, rv, rg, ret_norm, batch, seq):
    n, hk = rq.shape
    hv = rv.shape[1]
    dk, dv = hk // N_RET_HEADS, hv // N_RET_HEADS
    chunk = RET_CHUNK if seq % RET_CHUNK == 0 else seq
    nc = seq // chunk
    dmask, q_dec, k_dec, c_dec = _ret_tables(chunk, dk)
    row = lambda w: pl.BlockSpec((chunk, w), lambda b, c: (b * nc + c, 0))
    return pl.pallas_call(
        functools.partial(_ret_kernel, cdec=c_dec, dk=dk, dv=dv),
        grid=(batch, nc),
        in_specs=[row(hk), row(hk), row(hv), row(hv), _resident(dmask.shape), _resident(q_dec.shape),
                  _resident(k_dec.shape), _resident((1, hv))],
        out_specs=[row(hv), pl.BlockSpec((1, N_RET_HEADS, dk, dv), lambda b, c: (b, 0, 0, 0))],
        out_shape=[jax.ShapeDtypeStruct((n, hv), BF16),
                   jax.ShapeDtypeStruct((batch, N_RET_HEADS, dk, dv), F32)],
        compiler_params=_cparams("parallel", "arbitrary"),
        name="retention_prompt",
    )(rq, rk, rv, rg, jnp.asarray(dmask), jnp.asarray(q_dec), jnp.asarray(k_dec), ret_norm)


def _ret_dec_kernel(q_ref, k_ref, v_ref, g_ref, s_ref, dm_ref, qd_ref, kd_ref, rn_ref, a_ref, so_ref,
                    cross_ref, *, cdec, dk, dv, t_len):
    rows = q_ref.shape[0]
    nb = rows // t_len
    row_seq = lax.broadcasted_iota(jnp.int32, (rows, dk), 0) // t_len

    def body(b, carry):
        r0 = pl.multiple_of(b * t_len, t_len)
        for h in range(N_RET_HEADS):
            ks = slice(h * dk, (h + 1) * dk)
            vs = slice(h * dv, (h + 1) * dv)
            s = s_ref[b, h]
            qb = q_ref[pl.ds(r0, t_len), ks] * qd_ref[pl.ds(r0, t_len), ks]
            cross_ref[pl.ds(r0, t_len), vs] = jnp.dot(qb, s, preferred_element_type=F32)
            kb = jnp.where(row_seq == b, k_ref[:, ks] * kd_ref[:, ks], 0.0)
            so_ref[b, h] = s * cdec[h] + lax.dot_general(kb, v_ref[:, vs], _TN, preferred_element_type=F32)
        return carry

    lax.fori_loop(0, nb, body, 0)

    for h in range(N_RET_HEADS):
        ks = slice(h * dk, (h + 1) * dk)
        vs = slice(h * dv, (h + 1) * dv)
        att = lax.dot_general(q_ref[:, ks], k_ref[:, ks], _NT, preferred_element_type=F32) * dm_ref[h]
        inner = jnp.dot(att, v_ref[:, vs], preferred_element_type=F32)
        a_ref[:, vs] = _ret_out(inner + cross_ref[:, vs], g_ref[:, vs], rn_ref[:, vs]).astype(a_ref.dtype)


def _retention_sample(rq, rk, rv, rg, state, ret_norm, t_len, layer):
    n, hk = rq.shape
    hv = rv.shape[1]
    dk, dv = hk // N_RET_HEADS, hv // N_RET_HEADS
    nseq = n // t_len
    bb = min(DEC_BLOCK, nseq)
    rows = bb * t_len
    dmask, q_dec, k_dec, c_dec = _ret_tables(t_len, dk)
    same_seq = np.kron(np.eye(bb, dtype=np.float32), np.ones((t_len, t_len), np.float32))
    dmask = np.tile(dmask, (1, bb, bb)) * same_seq[None]
    q_dec, k_dec = np.tile(q_dec, (bb, 1)), np.tile(k_dec, (bb, 1))
    row = lambda w: pl.BlockSpec((rows, w), lambda i: (i, 0))
    nblk = nseq // bb
    st_shape = (bb, N_RET_HEADS, dk, dv)
    st_in = pl.BlockSpec(st_shape, lambda i: (layer * nblk + i, 0, 0, 0))
    st_out = pl.BlockSpec(st_shape, lambda i: (i, 0, 0, 0))
    state = state.reshape((-1,) + state.shape[2:])
    return pl.pallas_call(
        functools.partial(_ret_dec_kernel, cdec=c_dec, dk=dk, dv=dv, t_len=t_len),
        grid=(nblk,),
        in_specs=[row(hk), row(hk), row(hv), row(hv), st_in, _resident(dmask.shape), _resident(q_dec.shape),
                  _resident(k_dec.shape), _resident((1, hv))],
        out_specs=[row(hv), st_out],
        out_shape=[jax.ShapeDtypeStruct((n, hv), F32),
                   jax.ShapeDtypeStruct((nseq, N_RET_HEADS, dk, dv), F32)],
        scratch_shapes=[pltpu.VMEM((rows, hv), F32)],
        compiler_params=_cparams("parallel"),
        name="retention_sample",
    )(rq, rk, rv, rg, state, jnp.asarray(dmask), jnp.asarray(q_dec), jnp.asarray(k_dec), ret_norm)


def _lambda(lq1_ref, lk1_ref, lq2_ref, lk2_ref, lam_init):
    s1 = jnp.sum(lq1_ref[...] * lk1_ref[...], axis=-1, keepdims=True)
    s2 = jnp.sum(lq2_ref[...] * lk2_ref[...], axis=-1, keepdims=True)
    return jnp.exp(s1) - jnp.exp(s2) + lam_init


def _diff_out(o, norm_g, lam_init):
    r = lax.rsqrt(jnp.mean(o * o, axis=-1, keepdims=True) + NORM_EPS)
    return o * r * norm_g * (1.0 - lam_init)


def _flash_kernel(q_ref, k_ref, v_ref, lq1_ref, lk1_ref, lq2_ref, lk2_ref, dn_ref, o_ref, qs_ref, vx_ref,
                  s_ref, mx_ref, mb_ref, acc_ref, *, tq, hd, lam_init):
    i = pl.program_id(2)
    dv = v_ref.shape[1]
    n_lane = tq // LANES
    halves = (slice(0, tq), slice(tq, 2 * tq))

    @pl.when(i == 0)
    def _():
        vx_ref[:, 0:dv] = v_ref[...]
        vx_ref[:, dv:2 * dv] = jnp.ones(v_ref.shape, BF16)

    lam = _lambda(lq1_ref, lk1_ref, lq2_ref, lk2_ref, lam_init)
    q = q_ref[...]
    lane = lax.broadcasted_iota(jnp.int32, q.shape, 1)
    zero = jnp.zeros_like(q)
    qs_ref[0:tq, :] = jnp.where(lane < hd, q, zero)
    qs_ref[tq:2 * tq, :] = jnp.where(lane >= hd, q, zero)
    mx_ref[...] = jnp.full_like(mx_ref, NEG_INF)
    acc_ref[...] = jnp.zeros_like(acc_ref)

    def score_block(j, masked):
        kb = k_ref[pl.ds(pl.multiple_of(j * tq, tq), tq), :]
        for rows in halves:
            s = lax.dot_general(qs_ref[rows, :], kb, _NT, preferred_element_type=F32)
            if masked:
                tri = (lax.broadcasted_iota(jnp.int32, s.shape, 1)
                       <= lax.broadcasted_iota(jnp.int32, s.shape, 0))
                s = jnp.where(tri, s, NEG_INF)
            s_ref[j, rows, :] = s
            part = s[:, 0:LANES]
            for c in range(1, n_lane):
                part = jnp.maximum(part, s[:, c * LANES:(c + 1) * LANES])
            mx_ref[rows, :] = jnp.maximum(mx_ref[rows, :], part)

    def score_pair(t, carry):
        score_block(2 * t, False)
        score_block(2 * t + 1, False)
        return carry

    lax.fori_loop(0, i // 2, score_pair, 0)

    @pl.when(i % 2 == 1)
    def _():
        score_block(i - 1, False)
        score_block(i, True)

    @pl.when(i % 2 == 0)
    def _():
        score_block(i, True)

    mb_ref[...] = jnp.broadcast_to(jnp.max(mx_ref[...], axis=-1, keepdims=True), mb_ref.shape)

    def pv_blocks(js):
        for rows in halves:
            mb = mb_ref[rows, :]
            tot = None
            for j in js:
                p = jnp.concatenate(
                    [jnp.exp2(s_ref[j, rows, c * LANES:(c + 1) * LANES] - mb) for c in range(n_lane)],
                    axis=1).astype(BF16)
                d = jnp.dot(p, vx_ref[pl.ds(pl.multiple_of(j * tq, tq), tq), :], preferred_element_type=F32)
                tot = d if tot is None else tot + d
            acc_ref[rows, :] += tot

    def pv_pair(t, carry):
        pv_blocks((2 * t, 2 * t + 1))
        return carry

    lax.fori_loop(0, (i + 1) // 2, pv_pair, 0)

    @pl.when(i % 2 == 0)
    def _():
        pv_blocks((i,))

    acc = acc_ref[...]
    o = acc[:, 0:dv] / acc[:, dv:2 * dv]
    o_ref[...] = _diff_out(o[0:tq] - lam * o[tq:2 * tq], dn_ref[...], lam_init).astype(o_ref.dtype)


def _diff_attn_prompt(dq, dk, dv, lam_params, diff_norm, batch, seq, lam_init):
    n, hw = dq.shape
    hd = hw // (2 * N_DIFF_HEADS)
    w = 2 * hd
    tq = min(FLASH_TQ, seq)
    nq = seq // tq
    assert seq % tq == 0 and tq % LANES == 0
    qrow = pl.BlockSpec((tq, w), lambda b, h, i: (b * nq + i, h))
    kv = pl.BlockSpec((seq, w), lambda b, h, i: (b, h))
    return pl.pallas_call(
        functools.partial(_flash_kernel, tq=tq, hd=hd, lam_init=lam_init),
        grid=(batch, N_DIFF_HEADS, nq),
        in_specs=[qrow, kv, kv] + [_resident((1, hd))] * 4 + [_resident((1, w))],
        out_specs=qrow,
        out_shape=jax.ShapeDtypeStruct((n, hw), BF16),
        scratch_shapes=[pltpu.VMEM((2 * tq, w), BF16), pltpu.VMEM((seq, 2 * w), BF16),
                        pltpu.VMEM((nq, 2 * tq, tq), F32), pltpu.VMEM((2 * tq, LANES), F32),
                        pltpu.VMEM((2 * tq, LANES), F32), pltpu.VMEM((2 * tq, 2 * w), F32)],
        compiler_params=_cparams("parallel", "parallel", "arbitrary"),
        name="diff_attn_prompt",
    )(dq, dk, dv, *lam_params, diff_norm)


def _paged_kernel(pt_ref, q_ref, kn_ref, vn_ref, lq1_ref, lk1_ref, lq2_ref, lk2_ref, dn_ref, *rest,
                  n_pp, t_len, hd, lam_init):
    k_refs, v_refs = rest[:n_pp], rest[n_pp:2 * n_pp]
    o_ref, qbd_ref, m_ref, l_ref, acc_ref = rest[2 * n_pp:]
    p = pl.program_id(1)
    n_rows, width = qbd_ref.shape

    @pl.when(p == 0)
    def _():
        q = jnp.tile(q_ref[...], (n_rows // t_len, 1))
        r = lax.broadcasted_iota(jnp.int32, q.shape, 0) // t_len
        c = lax.broadcasted_iota(jnp.int32, q.shape, 1) // hd
        qbd_ref[...] = jnp.where(r == c, q, 0.0).astype(BF16)
        m_ref[...] = jnp.full_like(m_ref, NEG_INF)
        l_ref[...] = jnp.zeros_like(l_ref)
        acc_ref[...] = jnp.zeros_like(acc_ref)

    qbd = qbd_ref[...]

    def update(s, v):
        m_prev = m_ref[...]
        m_new = jnp.maximum(m_prev, jnp.max(s, axis=-1, keepdims=True))
        corr = jnp.exp2(m_prev - m_new)
        pe = jnp.exp2(s - m_new)
        l_ref[...] = l_ref[...] * corr + jnp.sum(pe, axis=-1, keepdims=True)
        acc_ref[...] = acc_ref[...] * corr + jnp.dot(pe.astype(BF16), v, preferred_element_type=F32)
        m_ref[...] = m_new

    def page_rows(ref):
        page = ref.shape[0] // N_DIFF_HEADS
        return jnp.concatenate([ref[pl.ds(h, page, stride=N_DIFF_HEADS), :] for h in range(N_DIFF_HEADS)],
                               axis=1).astype(BF16)

    k_all = jnp.concatenate([page_rows(k_refs[r]) for r in range(n_pp)], axis=0)
    v_all = jnp.concatenate([page_rows(v_refs[r]) for r in range(n_pp)], axis=0)
    update(lax.dot_general(qbd, k_all, _NT, preferred_element_type=F32), v_all)

    @pl.when(p == pl.num_programs(1) - 1)
    def _():
        pad = jnp.zeros((LANES - t_len, width), F32)
        kn = jnp.concatenate([kn_ref[...], pad], axis=0).astype(BF16)
        vn = jnp.concatenate([vn_ref[...], pad], axis=0).astype(BF16)
        sn = lax.dot_general(qbd, kn, _NT, preferred_element_type=F32)
        qpos = lax.broadcasted_iota(jnp.int32, sn.shape, 0) % t_len
        kpos = lax.broadcasted_iota(jnp.int32, sn.shape, 1)
        sn = jnp.where(kpos <= qpos, sn, NEG_INF)
        update(sn, vn)
        o = acc_ref[...] / l_ref[...]
        lam = _lambda(lq1_ref, lk1_ref, lq2_ref, lk2_ref, lam_init)
        dvw = width // N_DIFF_HEADS
        for h in range(N_DIFF_HEADS):
            cs = slice(h * dvw, (h + 1) * dvw)
            o1 = o[(2 * h) * t_len:(2 * h + 1) * t_len, cs]
            o2 = o[(2 * h + 1) * t_len:(2 * h + 2) * t_len, cs]
            o_ref[:, cs] = _diff_out(o1 - lam * o2, dn_ref[...], lam_init).astype(o_ref.dtype)


def _diff_attn_sample(dq, dk, dv, cache_k, cache_v, page_table, lam_params, diff_norm, t_len, lam_init, layer):
    n, hw = dq.shape
    hd = hw // (2 * N_DIFF_HEADS)
    nseq, n_pages = page_table.shape
    depth, n_pool, page = cache_k.shape[:3]
    page_rows, w = page * N_DIFF_HEADS, hw // N_DIFF_HEADS
    ck = cache_k.reshape(depth * n_pool, page_rows, w)
    cv = cache_v.reshape(depth * n_pool, page_rows, w)
    page_table = page_table + layer * n_pool
    n_pp = math.gcd(PAGES_PER_STEP, n_pages)
    n_rows = 2 * N_DIFF_HEADS * t_len
    row = pl.BlockSpec((t_len, hw), lambda b, p, pt: (b, 0))
    const = lambda shape: pl.BlockSpec(shape, lambda b, p, pt: (0, 0))
    page_spec = lambda r: pl.BlockSpec((None, page_rows, w), lambda b, p, pt: (pt[b, p * n_pp + r], 0, 0))
    grid_spec = pltpu.PrefetchScalarGridSpec(
        num_scalar_prefetch=1,
        grid=(nseq, n_pages // n_pp),
        in_specs=[row, row, row] + [const((1, hd))] * 4 + [const((1, 2 * hd))]
        + [page_spec(r) for r in range(n_pp)] * 2,
        out_specs=row,
        scratch_shapes=[pltpu.VMEM((n_rows, hw), BF16), pltpu.VMEM((n_rows, 1), F32),
                        pltpu.VMEM((n_rows, 1), F32), pltpu.VMEM((n_rows, hw), F32)],
    )
    return pl.pallas_call(
        functools.partial(_paged_kernel, n_pp=n_pp, t_len=t_len, hd=hd, lam_init=lam_init),
        grid_spec=grid_spec,
        out_shape=jax.ShapeDtypeStruct((n, hw), F32),
        compiler_params=_cparams("parallel", "arbitrary"),
        name="diff_attn_sample",
    )(page_table, dq, dk, dv, *lam_params, diff_norm, *([ck] * n_pp), *([cv] * n_pp))


def _merge_kernel(x_ref, ar_ref, ad_ref, gr_ref, gd_ref, wr_ref, wd_ref, wo_ref, o_ref):
    y_ret = jnp.dot(ar_ref[...].astype(BF16), wr_ref[...], preferred_element_type=F32)
    y_diff = jnp.dot(ad_ref[...].astype(BF16), wd_ref[...], preferred_element_type=F32)
    merged = gr_ref[...].astype(F32) * y_ret + gd_ref[...].astype(F32) * y_diff
    o_ref[...] = x_ref[...] + jnp.dot(merged.astype(BF16), wo_ref[...], preferred_element_type=F32)


def _merge(x, a_ret, a_diff, gr, gd, wr, wd, wo):
    n, d = x.shape
    tm = min(MERGE_TM, n)
    row = pl.BlockSpec((tm, d), lambda i: (i, 0))
    return pl.pallas_call(
        _merge_kernel,
        grid=(n // tm,),
        in_specs=[row] * 5 + [_resident(wr.shape), _resident(wd.shape), _resident(wo.shape)],
        out_specs=row,
        out_shape=jax.ShapeDtypeStruct((n, d), F32),
        compiler_params=_cparams("parallel"),
        name="merge",
    )(x, a_ret, a_diff, gr, gd, wr, wd, wo)


def _rotary_tables(pos, dk, hd):
    def cs(half):
        inv = ROPE_THETA ** (-jnp.arange(half, dtype=F32) / half)
        ang = pos[:, None] * inv[None, :]
        return jnp.cos(ang), jnp.sin(ang)

    cr, sr = cs(dk // 2)
    cd, sd = cs(hd // 2)
    zd = jnp.zeros_like(sd)
    reps = LANES // hd
    return (jnp.concatenate([cr, cr], axis=1), jnp.concatenate([-sr, sr], axis=1),
            jnp.tile(cd, (1, 2 * reps)), jnp.tile(jnp.concatenate([-sd, zd], axis=1), (1, reps)),
            jnp.tile(jnp.concatenate([zd, sd], axis=1), (1, reps)))


def _chunk_cols(w, fc):
    d, f = w.shape
    return w.reshape(d, f // fc, fc).transpose(1, 0, 2).astype(BF16)


def kernel(x_prompt, x_sample, cache_k, cache_v, state_ret, page_table, norm_ffn1, ffn1_w1, ffn1_w3, ffn1_w2,
           norm_mix, w_in, ret_norm, w_ret_proj, q_norm, k_norm, lambda_q1, lambda_k1, lambda_q2, lambda_k2,
           diff_norm, w_diff_proj, w_o, norm_ffn2, ffn2_w1, ffn2_w3, ffn2_w2):
    B, S, D = x_prompt.shape
    DB, T, _ = x_sample.shape
    depth = w_in.shape[0]
    hd = D // (2 * N_DIFF_HEADS)
    dk = D // (2 * N_RET_HEADS)
    assert dk == LANES and 2 * hd == LANES, "head layouts assume 128-lane retention heads and 64-lane diff heads"
    assert D % MXU_DIM == 0 and ffn1_w1.shape[-1] % FFN_CHUNK == 0
    past_len = page_table.shape[1] * cache_k.shape[2]

    tabs_p = _rotary_tables(jnp.arange(S, dtype=F32), dk, hd)
    mix_rows_s = min(MIX_TM, DB * T)
    tabs_s = tuple(jnp.tile(t, (mix_rows_s // T, 1))
                   for t in _rotary_tables(past_len + jnp.arange(T, dtype=F32), dk, hd))
    blk = np.arange(MXU_DIM) // hd
    bd = jnp.asarray(blk[:, None] == blk[None, :], dtype=BF16)

    xp = x_prompt.reshape(B * S, D)
    xs = x_sample.reshape(DB * T, D)
    outs = [[] for _ in range(6)]
    for l in range(depth):
        lam_init = 0.8 - 0.6 * math.exp(-0.3 * l)
        row = lambda a: a[l].reshape(1, -1)
        ffn1 = (row(norm_ffn1), _chunk_cols(ffn1_w1[l], FFN_CHUNK), _chunk_cols(ffn1_w3[l], FFN_CHUNK),
                ffn1_w2[l].reshape(-1, FFN_CHUNK, D).astype(BF16))
        ffn2 = (row(norm_ffn2), _chunk_cols(ffn2_w1[l], FFN_CHUNK), _chunk_cols(ffn2_w3[l], FFN_CHUNK),
                ffn2_w2[l].reshape(-1, FFN_CHUNK, D).astype(BF16))
        w_in_b = w_in[l].astype(BF16)
        qg = jnp.tile(row(q_norm), (1, LANES // hd))
        kg = jnp.tile(row(k_norm), (1, LANES // hd))
        lam_params = (row(lambda_q1), row(lambda_k1), row(lambda_q2), row(lambda_k2))
        rn, dn = row(ret_norm), row(diff_norm)
        proj = (w_ret_proj[l].astype(BF16), w_diff_proj[l].astype(BF16), w_o[l].astype(BF16))

        x1 = _ffn_half(xp, *ffn1)
        rq, rk, rv, rg, dq, dk32, dkb, dv32, dvb, gr, gd = _mixer_in(
            x1, row(norm_mix), w_in_b, tabs_p, qg, kg, bd, BF16, hd)
        a_ret, s_p = _retention_prompt(rq, rk, rv, rg, rn, B, S)
        a_diff = _diff_attn_prompt(dq, dkb, dvb, lam_params, dn, B, S, lam_init)
        xp = _ffn_half(_merge(x1, a_ret, a_diff, gr, gd, *proj), *ffn2)
        outs[0].append(dk32.reshape(B, S, N_DIFF_HEADS, 2 * hd))
        outs[1].append(dv32.reshape(B, S, N_DIFF_HEADS, 2 * hd))
        outs[2].append(s_p)

        x1 = _ffn_half(xs, *ffn1)
        rq, rk, rv, rg, dq, dk32, _, dv32, _, gr, gd = _mixer_in(
            x1, row(norm_mix), w_in_b, tabs_s, qg, kg, bd, F32, hd)
        a_ret, s_s = _retention_sample(rq, rk, rv, rg, state_ret, rn, T, l)
        a_diff = _diff_attn_sample(dq, dk32, dv32, cache_k, cache_v, page_table, lam_params, dn, T,
                                   lam_init, l)
        xs = _ffn_half(_merge(x1, a_ret, a_diff, gr, gd, *proj), *ffn2)
        outs[3].append(dk32.reshape(DB, T, N_DIFF_HEADS, 2 * hd))
        outs[4].append(dv32.reshape(DB, T, N_DIFF_HEADS, 2 * hd))
        outs[5].append(s_s)

    kp, vp, sp, ks, vs, ss = (jnp.stack(o) for o in outs)
    return xp.reshape(B, S, D), xs.reshape(DB, T, D), kp, vp, sp, ks, vs, ss
```

```python
import functools
import math

import numpy as np
import jax
import jax.numpy as jnp
from jax import lax
from jax.experimental import pallas as pl
from jax.experimental.pallas import tpu as pltpu

F32 = jnp.float32
BF16 = jnp.bfloat16

N_RET_HEADS = 4
N_DIFF_HEADS = 8
RET_CHUNK = 128
RET_STEP_CHUNKS = 4
ROPE_THETA = 10000.0
NORM_EPS = 1e-6
NEG_INF = -1e30

LANES = 128
MXU_DIM = 256
VMEM_LIMIT = 56 * 1024 * 1024

FFN_TM = 512
FFN_CHUNK = MXU_DIM
MIX_TM = 256
MERGE_TM = 512
FLASH_TQ = 512
PAGES_PER_STEP = 16
LOG2E = math.log2(math.e)
DEC_BLOCK = 16

_NT = (((1,), (1,)), ((), ()))
_TN = (((0,), (0,)), ((), ()))


def _cparams(*sem):
    return pltpu.CompilerParams(dimension_semantics=sem, vmem_limit_bytes=VMEM_LIMIT)


def _resident(shape):
    nd = len(shape)
    return pl.BlockSpec(shape, lambda *_: (0,) * nd, pipeline_mode=pl.Buffered(1))


def _silu(x):
    return x * jax.nn.sigmoid(x)


def _ffn_kernel(x_ref, g_ref, w1_ref, w3_ref, w2_ref, o_ref, h_ref, acc_ref, *, n_chunks):
    x = x_ref[...]
    r = lax.rsqrt(jnp.mean(x * x, axis=-1, keepdims=True) + NORM_EPS)
    h_ref[...] = (x * r * g_ref[...]).astype(BF16)

    for c in range(n_chunks):
        cols = slice(c * FFN_CHUNK, (c + 1) * FFN_CHUNK)
        h = h_ref[...]
        a = jnp.dot(h, w1_ref[:, cols], preferred_element_type=F32)
        b = jnp.dot(h, w3_ref[:, cols], preferred_element_type=F32)
        u = (_silu(a) * b).astype(BF16)
        d = jnp.dot(u, w2_ref[cols, :], preferred_element_type=F32)
        if c == 0:
            acc_ref[...] = d
        elif c < n_chunks - 1:
            acc_ref[...] += d
        else:
            o_ref[...] = x_ref[...] + 0.5 * (acc_ref[...] + d)


def _ffn_half(x, g, w1, w3, w2):
    n, d = x.shape
    f = w1.shape[1]
    tm = min(FFN_TM, n)
    row = pl.BlockSpec((tm, d), lambda i: (i, 0))
    return pl.pallas_call(
        functools.partial(_ffn_kernel, n_chunks=f // FFN_CHUNK),
        grid=(n // tm,),
        in_specs=[row, _resident((1, d)), _resident((d, f)), _resident((d, f)), _resident((f, d))],
        out_specs=row,
        out_shape=jax.ShapeDtypeStruct((n, d), F32),
        scratch_shapes=[pltpu.VMEM((tm, d), BF16), pltpu.VMEM((tm, d), F32)],
        compiler_params=_cparams("parallel"),
        name="ffn_half",
    )(x, g, w1, w3, w2)


def _mix_kernel(x_ref, g_ref, w_ref, cosr_ref, sinr_ref, cosd_ref, sina_ref, sinb_ref, qg_ref, kg_ref,
                bd_ref, rq_o, rk_o, rv_o, rg_o, dq_o, dk_o, dkb_o, dv_o, dvb_o, gr_o, gd_o, *, d, hd):
    x = x_ref[...]
    r = lax.rsqrt(jnp.mean(x * x, axis=-1, keepdims=True) + NORM_EPS)
    h = (x * r * g_ref[...]).astype(BF16)

    def proj(c0, width):
        return jnp.dot(h, w_ref[:, c0:c0 + width], preferred_element_type=F32)

    rw = d // 2
    cosr, sinr = cosr_ref[...], sinr_ref[...]
    for o_ref, c0, scale in ((rq_o, 0, None), (rk_o, rw, LANES ** -0.5)):
        z = proj(c0, rw)
        for hh in range(rw // LANES):
            zh = z[:, hh * LANES:(hh + 1) * LANES]
            y = zh * cosr + pltpu.roll(zh, LANES // 2, 1) * sinr
            if scale is not None:
                y = y * scale
            o_ref[:, hh * LANES:(hh + 1) * LANES] = y.astype(o_ref.dtype)

    rv_o[...] = proj(2 * rw, d).astype(rv_o.dtype)
    rg_o[...] = proj(2 * rw + d, d).astype(rg_o.dtype)

    cosd, sina, sinb = cosd_ref[...], sina_ref[...], sinb_ref[...]
    bd = bd_ref[...]
    c_dq = 2 * rw + 2 * d
    for c0, gain_ref, is_q in ((c_dq, qg_ref, True), (c_dq + d, kg_ref, False)):
        z = proj(c0, d)
        gain = gain_ref[...]
        for cc in range(d // MXU_DIM):
            zc = z[:, cc * MXU_DIM:(cc + 1) * MXU_DIM]
            ss = jnp.dot((zc * zc).astype(BF16), bd, preferred_element_type=F32)
            yn = zc * lax.rsqrt(ss * (1.0 / hd) + NORM_EPS)
            for s2 in range(MXU_DIM // LANES):
                y = yn[:, s2 * LANES:(s2 + 1) * LANES] * gain
                y = (y * cosd + pltpu.roll(y, LANES - hd // 2, 1) * sina
                     + pltpu.roll(y, hd // 2, 1) * sinb)
                col = cc * MXU_DIM + s2 * LANES
                if is_q:
                    dq_o[:, col:col + LANES] = (y * (hd ** -0.5 * LOG2E)).astype(dq_o.dtype)
                else:
                    dk_o[:, col:col + LANES] = y
                    dkb_o[:, col:col + LANES] = y.astype(dkb_o.dtype)

    z = proj(c_dq + 2 * d, d)
    dv_o[...] = z
    dvb_o[...] = z.astype(dvb_o.dtype)
    gr_o[...] = jax.nn.sigmoid(proj(c_dq + 3 * d, d)).astype(gr_o.dtype)
    gd_o[...] = jax.nn.sigmoid(proj(c_dq + 4 * d, d)).astype(gd_o.dtype)


def _mixer_in(x, g, w_in, tabs, qg, kg, bd, act_dtype, hd):
    n, d = x.shape
    tm = min(MIX_TM, n)
    n_tab = tabs[0].shape[0] // tm
    row = lambda w: pl.BlockSpec((tm, w), lambda i: (i, 0))
    tab = pl.BlockSpec((tm, LANES), lambda i: (i % n_tab, 0))
    sds = lambda w, dt: jax.ShapeDtypeStruct((n, w), dt)
    widths = (d // 2, d // 2, d, d, d, d, d, d, d, d, d)
    dtypes = (act_dtype, act_dtype, act_dtype, act_dtype, act_dtype, F32, act_dtype, F32, act_dtype,
              act_dtype, act_dtype)
    return pl.pallas_call(
        functools.partial(_mix_kernel, d=d, hd=hd),
        grid=(n // tm,),
        in_specs=[row(d), _resident((1, d)), _resident(w_in.shape), tab, tab, tab, tab, tab,
                  _resident((1, LANES)), _resident((1, LANES)), _resident((MXU_DIM, MXU_DIM))],
        out_specs=[row(w) for w in widths],
        out_shape=[sds(w, dt) for w, dt in zip(widths, dtypes)],
        compiler_params=_cparams("parallel"),
        name="mixer_in",
    )(x, g, w_in, *tabs, qg, kg, bd)


def _ret_out(o, gate, norm_g):
    r = lax.rsqrt(jnp.mean(o * o, axis=-1, keepdims=True) + NORM_EPS)
    return _silu(gate) * (o * r * norm_g)


def _ret_kernel(q_ref, k_ref, v_ref, g_ref, dm_ref, qd_ref, kd_ref, rn_ref, a_ref, s_ref, *, cdec, dk, dv):
    @pl.when(pl.program_id(1) == 0)
    def _():
        s_ref[...] = jnp.zeros_like(s_ref)

    chunk = dm_ref.shape[1]
    for c0 in range(0, q_ref.shape[0], chunk):
        rs = slice(c0, c0 + chunk)
        for h in range(N_RET_HEADS):
            ks = slice(h * dk, (h + 1) * dk)
            vs = slice(h * dv, (h + 1) * dv)
            q, k, v = q_ref[rs, ks], k_ref[rs, ks], v_ref[rs, vs]
            att = lax.dot_general(q, k, _NT, preferred_element_type=F32) * dm_ref[h]
            inner = jnp.dot(att.astype(BF16), v, preferred_element_type=F32)
            s = s_ref[0, h]
            qdec = (q.astype(F32) * qd_ref[:, ks]).astype(BF16)
            cross = jnp.dot(qdec, s.astype(BF16), preferred_element_type=F32)
            kdec = (k.astype(F32) * kd_ref[:, ks]).astype(BF16)
            s_ref[0, h] = s * cdec[h] + lax.dot_general(kdec, v, _TN, preferred_element_type=F32)
            a_ref[rs, vs] = _ret_out(inner + cross, g_ref[rs, vs].astype(F32),
                                     rn_ref[:, vs]).astype(a_ref.dtype)


def _ret_tables(chunk, dk):
    h = np.arange(N_RET_HEADS, dtype=np.float64)
    lg = np.log1p(-(2.0 ** (-5.0 - h)))
    idx = np.arange(chunk, dtype=np.float64)
    diff = idx[:, None] - idx[None, :]
    dmask = np.where(diff >= 0, np.exp(np.maximum(diff, 0.0)[None] * lg[:, None, None]), 0.0)
    q_dec = np.exp((idx[:, None] + 1.0) * lg[None, :])
    k_dec = np.exp((chunk - 1.0 - idx)[:, None] * lg[None, :])
    c_dec = tuple(float(c) for c in np.exp(chunk * lg))
    rep = lambda t: np.repeat(t, dk, axis=1)
    return dmask.astype(np.float32), rep(q_dec).astype(np.float32), rep(k_dec).astype(np.float32), c_dec


def _retention_prompt(rq, rk, rv, rg, ret_norm, batch, seq):
    n, hk = rq.shape
    hv = rv.shape[1]
    dk, dv = hk // N_RET_HEADS, hv // N_RET_HEADS
    chunk = RET_CHUNK if seq % RET_CHUNK == 0 else seq
    nc = seq // chunk
    dmask, q_dec, k_dec, c_dec = _ret_tables(chunk, dk)
    per_step = math.gcd(RET_STEP_CHUNKS, nc)
    ns = nc // per_step
    row = lambda w: pl.BlockSpec((per_step * chunk, w), lambda b, c: (b * ns + c, 0))
    return pl.pallas_call(
        functools.partial(_ret_kernel, cdec=c_dec, dk=dk, dv=dv),
        grid=(batch, ns),
        in_specs=[row(hk), row(hk), row(hv), row(hv), _resident(dmask.shape), _resident(q_dec.shape),
                  _resident(k_dec.shape), _resident((1, hv))],
        out_specs=[row(hv), pl.BlockSpec((1, N_RET_HEADS, dk, dv), lambda b, c: (b, 0, 0, 0))],
        out_shape=[jax.ShapeDtypeStruct((n, hv), BF16),
                   jax.ShapeDtypeStruct((batch, N_RET_HEADS, dk, dv), F32)],
        compiler_params=_cparams("parallel", "arbitrary"),
        name="retention_prompt",
    )(rq, rk, rv, rg, jnp.asarray(dmask), jnp.asarray(q_dec), jnp.asarray(k_dec), ret_norm)


def _ret_dec_kernel(q_ref, k_ref, v_ref, g_ref, s_ref, dm_ref, qd_ref, kd_ref, rn_ref, a_ref, so_ref,
                    cross_ref, *, cdec, dk, dv, t_len):
    rows = q_ref.shape[0]
    nb = rows // t_len
    row_seq = lax.broadcasted_iota(jnp.int32, (rows, dk), 0) // t_len

    def body(b, carry):
        r0 = pl.multiple_of(b * t_len, t_len)
        for h in range(N_RET_HEADS):
            ks = slice(h * dk, (h + 1) * dk)
            vs = slice(h * dv, (h + 1) * dv)
            s = s_ref[b, h]
            qb = q_ref[pl.ds(r0, t_len), ks] * qd_ref[pl.ds(r0, t_len), ks]
            cross_ref[pl.ds(r0, t_len), vs] = jnp.dot(qb, s, preferred_element_type=F32)
            kb = jnp.where(row_seq == b, k_ref[:, ks] * kd_ref[:, ks], 0.0)
            so_ref[b, h] = s * cdec[h] + lax.dot_general(kb, v_ref[:, vs], _TN, preferred_element_type=F32)
        return carry

    lax.fori_loop(0, nb, body, 0)

    for h in range(N_RET_HEADS):
        ks = slice(h * dk, (h + 1) * dk)
        vs = slice(h * dv, (h + 1) * dv)
        att = lax.dot_general(q_ref[:, ks], k_ref[:, ks], _NT, preferred_element_type=F32) * dm_ref[h]
        inner = jnp.dot(att, v_ref[:, vs], preferred_element_type=F32)
        a_ref[:, vs] = _ret_out(inner + cross_ref[:, vs], g_ref[:, vs], rn_ref[:, vs]).astype(a_ref.dtype)


def _retention_sample(rq, rk, rv, rg, state, ret_norm, t_len, layer):
    n, hk = rq.shape
    hv = rv.shape[1]
    dk, dv = hk // N_RET_HEADS, hv // N_RET_HEADS
    nseq = n // t_len
    bb = min(DEC_BLOCK, nseq)
    rows = bb * t_len
    dmask, q_dec, k_dec, c_dec = _ret_tables(t_len, dk)
    same_seq = np.kron(np.eye(bb, dtype=np.float32), np.ones((t_len, t_len), np.float32))
    dmask = np.tile(dmask, (1, bb, bb)) * same_seq[None]
    q_dec, k_dec = np.tile(q_dec, (bb, 1)), np.tile(k_dec, (bb, 1))
    row = lambda w: pl.BlockSpec((rows, w), lambda i: (i, 0))
    nblk = nseq // bb
    st_shape = (bb, N_RET_HEADS, dk, dv)
    st_in = pl.BlockSpec(st_shape, lambda i: (layer * nblk + i, 0, 0, 0))
    st_out = pl.BlockSpec(st_shape, lambda i: (i, 0, 0, 0))
    state = state.reshape((-1,) + state.shape[2:])
    return pl.pallas_call(
        functools.partial(_ret_dec_kernel, cdec=c_dec, dk=dk, dv=dv, t_len=t_len),
        grid=(nblk,),
        in_specs=[row(hk), row(hk), row(hv), row(hv), st_in, _resident(dmask.shape), _resident(q_dec.shape),
                  _resident(k_dec.shape), _resident((1, hv))],
        out_specs=[row(hv), st_out],
        out_shape=[jax.ShapeDtypeStruct((n, hv), F32),
                   jax.ShapeDtypeStruct((nseq, N_RET_HEADS, dk, dv), F32)],
        scratch_shapes=[pltpu.VMEM((rows, hv), F32)],
        compiler_params=_cparams("parallel"),
        name="retention_sample",
    )(rq, rk, rv, rg, state, jnp.asarray(dmask), jnp.asarray(q_dec), jnp.asarray(k_dec), ret_norm)


def _lambda(lq1_ref, lk1_ref, lq2_ref, lk2_ref, lam_init):
    s1 = jnp.sum(lq1_ref[...] * lk1_ref[...], axis=-1, keepdims=True)
    s2 = jnp.sum(lq2_ref[...] * lk2_ref[...], axis=-1, keepdims=True)
    return jnp.exp(s1) - jnp.exp(s2) + lam_init


def _diff_out(o, norm_g, lam_init):
    r = lax.rsqrt(jnp.mean(o * o, axis=-1, keepdims=True) + NORM_EPS)
    return o * r * norm_g * (1.0 - lam_init)


def _flash_kernel(q_ref, k_ref, v_ref, lq1_ref, lk1_ref, lq2_ref, lk2_ref, dn_ref, o_ref, qs_ref, vx_ref,
                  s_ref, mx_ref, mb_ref, acc_ref, *, tq, hd, lam_init):
    dv = v_ref.shape[1]
    n_lane = tq // LANES
    halves = (slice(0, tq), slice(tq, 2 * tq))

    vx_ref[:, 0:dv] = v_ref[...]
    vx_ref[:, dv:2 * dv] = jnp.ones(v_ref.shape, BF16)
    lam = _lambda(lq1_ref, lk1_ref, lq2_ref, lk2_ref, lam_init)

    def query_tile(i, tile_carry):
        q_rows = pl.ds(pl.multiple_of(i * tq, tq), tq)
        q = q_ref[q_rows, :]
        lane = lax.broadcasted_iota(jnp.int32, q.shape, 1)
        zero = jnp.zeros_like(q)
        qs_ref[0:tq, :] = jnp.where(lane < hd, q, zero)
        qs_ref[tq:2 * tq, :] = jnp.where(lane >= hd, q, zero)
        mx_ref[...] = jnp.full_like(mx_ref, NEG_INF)
        acc_ref[...] = jnp.zeros_like(acc_ref)

        def score_block(j, masked):
            kb = k_ref[pl.ds(pl.multiple_of(j * tq, tq), tq), :]
            for rows in halves:
                s = lax.dot_general(qs_ref[rows, :], kb, _NT, preferred_element_type=F32)
                if masked:
                    tri = (lax.broadcasted_iota(jnp.int32, s.shape, 1)
                           <= lax.broadcasted_iota(jnp.int32, s.shape, 0))
                    s = jnp.where(tri, s, NEG_INF)
                s_ref[j, rows, :] = s
                part = s[:, 0:LANES]
                for c in range(1, n_lane):
                    part = jnp.maximum(part, s[:, c * LANES:(c + 1) * LANES])
                mx_ref[rows, :] = jnp.maximum(mx_ref[rows, :], part)

        def score_pair(t, carry):
            score_block(2 * t, False)
            score_block(2 * t + 1, False)
            return carry

        lax.fori_loop(0, i // 2, score_pair, 0)

        @pl.when(i % 2 == 1)
        def _():
            score_block(i - 1, False)
            score_block(i, True)

        @pl.when(i % 2 == 0)
        def _():
            score_block(i, True)

        mb_ref[...] = jnp.broadcast_to(jnp.max(mx_ref[...], axis=-1, keepdims=True), mb_ref.shape)

        def pv_blocks(js):
            for rows in halves:
                mb = mb_ref[rows, :]
                tot = None
                for j in js:
                    p = jnp.concatenate(
                        [jnp.exp2(s_ref[j, rows, c * LANES:(c + 1) * LANES] - mb) for c in range(n_lane)],
                        axis=1).astype(BF16)
                    vb = vx_ref[pl.ds(pl.multiple_of(j * tq, tq), tq), :]
                    d = jnp.dot(p, vb, preferred_element_type=F32)
                    tot = d if tot is None else tot + d
                acc_ref[rows, :] += tot

        def pv_pair(t, carry):
            pv_blocks((2 * t, 2 * t + 1))
            return carry

        lax.fori_loop(0, (i + 1) // 2, pv_pair, 0)

        @pl.when(i % 2 == 0)
        def _():
            pv_blocks((i,))

        acc = acc_ref[...]
        o = acc[:, 0:dv] / acc[:, dv:2 * dv]
        o_ref[q_rows, :] = _diff_out(o[0:tq] - lam * o[tq:2 * tq], dn_ref[...], lam_init).astype(o_ref.dtype)
        return tile_carry

    lax.fori_loop(0, q_ref.shape[0] // tq, query_tile, 0)


def _diff_attn_prompt(dq, dk, dv, lam_params, diff_norm, batch, seq, lam_init):
    n, hw = dq.shape
    hd = hw // (2 * N_DIFF_HEADS)
    w = 2 * hd
    tq = min(FLASH_TQ, seq)
    nq = seq // tq
    assert seq % tq == 0 and tq % LANES == 0
    kv = pl.BlockSpec((seq, w), lambda b, h: (b, h))
    return pl.pallas_call(
        functools.partial(_flash_kernel, tq=tq, hd=hd, lam_init=lam_init),
        grid=(batch, N_DIFF_HEADS),
        in_specs=[kv, kv, kv] + [_resident((1, hd))] * 4 + [_resident((1, w))],
        out_specs=kv,
        out_shape=jax.ShapeDtypeStruct((n, hw), BF16),
        scratch_shapes=[pltpu.VMEM((2 * tq, w), BF16), pltpu.VMEM((seq, 2 * w), BF16),
                        pltpu.VMEM((nq, 2 * tq, tq), F32), pltpu.VMEM((2 * tq, LANES), F32),
                        pltpu.VMEM((2 * tq, LANES), F32), pltpu.VMEM((2 * tq, 2 * w), F32)],
        compiler_params=_cparams("parallel", "parallel"),
        name="diff_attn_prompt",
    )(dq, dk, dv, *lam_params, diff_norm)


def _paged_kernel(pt_ref, q_ref, kn_ref, vn_ref, lq1_ref, lk1_ref, lq2_ref, lk2_ref, dn_ref, *rest,
                  n_pp, t_len, hd, lam_init):
    k_refs, v_refs = rest[:n_pp], rest[n_pp:2 * n_pp]
    o_ref, qbd_ref, m_ref, l_ref, acc_ref = rest[2 * n_pp:]
    p = pl.program_id(1)
    n_rows, width = qbd_ref.shape

    @pl.when(p == 0)
    def _():
        q = jnp.tile(q_ref[...], (n_rows // t_len, 1))
        r = lax.broadcasted_iota(jnp.int32, q.shape, 0) // t_len
        c = lax.broadcasted_iota(jnp.int32, q.shape, 1) // hd
        qbd_ref[...] = jnp.where(r == c, q, 0.0).astype(BF16)
        m_ref[...] = jnp.full_like(m_ref, NEG_INF)
        l_ref[...] = jnp.zeros_like(l_ref)
        acc_ref[...] = jnp.zeros_like(acc_ref)

    qbd = qbd_ref[...]

    def update(s, v):
        m_prev = m_ref[...]
        m_new = jnp.maximum(m_prev, jnp.max(s, axis=-1, keepdims=True))
        corr = jnp.exp2(m_prev - m_new)
        pe = jnp.exp2(s - m_new)
        l_ref[...] = l_ref[...] * corr + jnp.sum(pe, axis=-1, keepdims=True)
        acc_ref[...] = acc_ref[...] * corr + jnp.dot(pe.astype(BF16), v, preferred_element_type=F32)
        m_ref[...] = m_new

    def page_rows(ref):
        page = ref.shape[0] // N_DIFF_HEADS
        return jnp.concatenate([ref[pl.ds(h, page, stride=N_DIFF_HEADS), :] for h in range(N_DIFF_HEADS)],
                               axis=1).astype(BF16)

    k_all = jnp.concatenate([page_rows(k_refs[r]) for r in range(n_pp)], axis=0)
    v_all = jnp.concatenate([page_rows(v_refs[r]) for r in range(n_pp)], axis=0)
    update(lax.dot_general(qbd, k_all, _NT, preferred_element_type=F32), v_all)

    @pl.when(p == pl.num_programs(1) - 1)
    def _():
        pad = jnp.zeros((LANES - t_len, width), F32)
        kn = jnp.concatenate([kn_ref[...], pad], axis=0).astype(BF16)
        vn = jnp.concatenate([vn_ref[...], pad], axis=0).astype(BF16)
        sn = lax.dot_general(qbd, kn, _NT, preferred_element_type=F32)
        qpos = lax.broadcasted_iota(jnp.int32, sn.shape, 0) % t_len
        kpos = lax.broadcasted_iota(jnp.int32, sn.shape, 1)
        sn = jnp.where(kpos <= qpos, sn, NEG_INF)
        update(sn, vn)
        o = acc_ref[...] / l_ref[...]
        lam = _lambda(lq1_ref, lk1_ref, lq2_ref, lk2_ref, lam_init)
        dvw = width // N_DIFF_HEADS
        for h in range(N_DIFF_HEADS):
            cs = slice(h * dvw, (h + 1) * dvw)
            o1 = o[(2 * h) * t_len:(2 * h + 1) * t_len, cs]
            o2 = o[(2 * h + 1) * t_len:(2 * h + 2) * t_len, cs]
            o_ref[:, cs] = _diff_out(o1 - lam * o2, dn_ref[...], lam_init).astype(o_ref.dtype)


def _diff_attn_sample(dq, dk, dv, cache_k, cache_v, page_table, lam_params, diff_norm, t_len, lam_init, layer):
    n, hw = dq.shape
    hd = hw // (2 * N_DIFF_HEADS)
    nseq, n_pages = page_table.shape
    depth, n_pool, page = cache_k.shape[:3]
    page_rows, w = page * N_DIFF_HEADS, hw // N_DIFF_HEADS
    ck = cache_k.reshape(depth * n_pool, page_rows, w)
    cv = cache_v.reshape(depth * n_pool, page_rows, w)
    page_table = page_table + layer * n_pool
    n_pp = math.gcd(PAGES_PER_STEP, n_pages)
    n_rows = 2 * N_DIFF_HEADS * t_len
    row = pl.BlockSpec((t_len, hw), lambda b, p, pt: (b, 0))
    const = lambda shape: pl.BlockSpec(shape, lambda b, p, pt: (0, 0))
    page_spec = lambda r: pl.BlockSpec((None, page_rows, w), lambda b, p, pt: (pt[b, p * n_pp + r], 0, 0))
    grid_spec = pltpu.PrefetchScalarGridSpec(
        num_scalar_prefetch=1,
        grid=(nseq, n_pages // n_pp),
        in_specs=[row, row, row] + [const((1, hd))] * 4 + [const((1, 2 * hd))]
        + [page_spec(r) for r in range(n_pp)] * 2,
        out_specs=row,
        scratch_shapes=[pltpu.VMEM((n_rows, hw), BF16), pltpu.VMEM((n_rows, 1), F32),
                        pltpu.VMEM((n_rows, 1), F32), pltpu.VMEM((n_rows, hw), F32)],
    )
    return pl.pallas_call(
        functools.partial(_paged_kernel, n_pp=n_pp, t_len=t_len, hd=hd, lam_init=lam_init),
        grid_spec=grid_spec,
        out_shape=jax.ShapeDtypeStruct((n, hw), F32),
        compiler_params=_cparams("parallel", "arbitrary"),
        name="diff_attn_sample",
    )(page_table, dq, dk, dv, *lam_params, diff_norm, *([ck] * n_pp), *([cv] * n_pp))


def _merge_kernel(x_ref, ar_ref, ad_ref, gr_ref, gd_ref, wr_ref, wd_ref, wo_ref, o_ref):
    y_ret = jnp.dot(ar_ref[...].astype(BF16), wr_ref[...], preferred_element_type=F32)
    y_diff = jnp.dot(ad_ref[...].astype(BF16), wd_ref[...], preferred_element_type=F32)
    merged = gr_ref[...].astype(F32) * y_ret + gd_ref[...].astype(F32) * y_diff
    o_ref[...] = x_ref[...] + jnp.dot(merged.astype(BF16), wo_ref[...], preferred_element_type=F32)


def _merge(x, a_ret, a_diff, gr, gd, wr, wd, wo):
    n, d = x.shape
    tm = min(MERGE_TM, n)
    row = pl.BlockSpec((tm, d), lambda i: (i, 0))
    return pl.pallas_call(
        _merge_kernel,
        grid=(n // tm,),
        in_specs=[row] * 5 + [_resident(wr.shape), _resident(wd.shape), _resident(wo.shape)],
        out_specs=row,
        out_shape=jax.ShapeDtypeStruct((n, d), F32),
        compiler_params=_cparams("parallel"),
        name="merge",
    )(x, a_ret, a_diff, gr, gd, wr, wd, wo)


def _rotary_tables(pos, dk, hd):
    def cs(half):
        inv = ROPE_THETA ** (-jnp.arange(half, dtype=F32) / half)
        ang = pos[:, None] * inv[None, :]
        return jnp.cos(ang), jnp.sin(ang)

    cr, sr = cs(dk // 2)
    cd, sd = cs(hd // 2)
    zd = jnp.zeros_like(sd)
    reps = LANES // hd
    return (jnp.concatenate([cr, cr], axis=1), jnp.concatenate([-sr, sr], axis=1),
            jnp.tile(cd, (1, 2 * reps)), jnp.tile(jnp.concatenate([-sd, zd], axis=1), (1, reps)),
            jnp.tile(jnp.concatenate([zd, sd], axis=1), (1, reps)))


def kernel(x_prompt, x_sample, cache_k, cache_v, state_ret, page_table, norm_ffn1, ffn1_w1, ffn1_w3, ffn1_w2,
           norm_mix, w_in, ret_norm, w_ret_proj, q_norm, k_norm, lambda_q1, lambda_k1, lambda_q2, lambda_k2,
           diff_norm, w_diff_proj, w_o, norm_ffn2, ffn2_w1, ffn2_w3, ffn2_w2):
    B, S, D = x_prompt.shape
    DB, T, _ = x_sample.shape
    depth = w_in.shape[0]
    hd = D // (2 * N_DIFF_HEADS)
    dk = D // (2 * N_RET_HEADS)
    assert dk == LANES and 2 * hd == LANES, "head layouts assume 128-lane retention heads and 64-lane diff heads"
    assert D % MXU_DIM == 0 and ffn1_w1.shape[-1] % FFN_CHUNK == 0
    past_len = page_table.shape[1] * cache_k.shape[2]

    tabs_p = _rotary_tables(jnp.arange(S, dtype=F32), dk, hd)
    mix_rows_s = min(MIX_TM, DB * T)
    tabs_s = tuple(jnp.tile(t, (mix_rows_s // T, 1))
                   for t in _rotary_tables(past_len + jnp.arange(T, dtype=F32), dk, hd))
    blk = np.arange(MXU_DIM) // hd
    bd = jnp.asarray(blk[:, None] == blk[None, :], dtype=BF16)

    xp = x_prompt.reshape(B * S, D)
    xs = x_sample.reshape(DB * T, D)
    outs = [[] for _ in range(6)]
    for l in range(depth):
        lam_init = 0.8 - 0.6 * math.exp(-0.3 * l)
        row = lambda a: a[l].reshape(1, -1)
        ffn1 = (row(norm_ffn1), ffn1_w1[l].astype(BF16), ffn1_w3[l].astype(BF16), ffn1_w2[l].astype(BF16))
        ffn2 = (row(norm_ffn2), ffn2_w1[l].astype(BF16), ffn2_w3[l].astype(BF16), ffn2_w2[l].astype(BF16))
        w_in_b = w_in[l].astype(BF16)
        qg = jnp.tile(row(q_norm), (1, LANES // hd))
        kg = jnp.tile(row(k_norm), (1, LANES // hd))
        lam_params = (row(lambda_q1), row(lambda_k1), row(lambda_q2), row(lambda_k2))
        rn, dn = row(ret_norm), row(diff_norm)
        proj = (w_ret_proj[l].astype(BF16), w_diff_proj[l].astype(BF16), w_o[l].astype(BF16))

        x1 = _ffn_half(xp, *ffn1)
        rq, rk, rv, rg, dq, dk32, dkb, dv32, dvb, gr, gd = _mixer_in(
            x1, row(norm_mix), w_in_b, tabs_p, qg, kg, bd, BF16, hd)
        a_ret, s_p = _retention_prompt(rq, rk, rv, rg, rn, B, S)
        a_diff = _diff_attn_prompt(dq, dkb, dvb, lam_params, dn, B, S, lam_init)
        xp = _ffn_half(_merge(x1, a_ret, a_diff, gr, gd, *proj), *ffn2)
        outs[0].append(dk32.reshape(B, S, N_DIFF_HEADS, 2 * hd))
        outs[1].append(dv32.reshape(B, S, N_DIFF_HEADS, 2 * hd))
        outs[2].append(s_p)

        x1 = _ffn_half(xs, *ffn1)
        rq, rk, rv, rg, dq, dk32, _, dv32, _, gr, gd = _mixer_in(
            x1, row(norm_mix), w_in_b, tabs_s, qg, kg, bd, F32, hd)
        a_ret, s_s = _retention_sample(rq, rk, rv, rg, state_ret, rn, T, l)
        a_diff = _diff_attn_sample(dq, dk32, dv32, cache_k, cache_v, page_table, lam_params, dn, T,
                                   lam_init, l)
        xs = _ffn_half(_merge(x1, a_ret, a_diff, gr, gd, *proj), *ffn2)
        outs[3].append(dk32.reshape(DB, T, N_DIFF_HEADS, 2 * hd))
        outs[4].append(dv32.reshape(DB, T, N_DIFF_HEADS, 2 * hd))
        outs[5].append(s_s)

    kp, vp, sp, ks, vs, ss = (jnp.stack(o) for o in outs)
    return xp.reshape(B, S, D), xs.reshape(DB, T, D), kp, vp, sp, ks, vs, ss
```

```python
import functools
import math

import numpy as np
import jax
import jax.numpy as jnp
from jax import lax
from jax.experimental import pallas as pl
from jax.experimental.pallas import tpu as pltpu

F32 = jnp.float32
BF16 = jnp.bfloat16

N_RET_HEADS = 4
N_DIFF_HEADS = 8
RET_CHUNK = 128
RET_STEP_CHUNKS = 4
ROPE_THETA = 10000.0
NORM_EPS = 1e-6
NEG_INF = -1e30

LANES = 128
MXU_DIM = 256
VMEM_LIMIT = 56 * 1024 * 1024

FFN_TM = 512
FFN_CHUNK = MXU_DIM
MIX_TM = 256
MERGE_TM = 512
FLASH_TQ = 512
PAGES_PER_STEP = 16
LOG2E = math.log2(math.e)
DEC_BLOCK = 16

_NT = (((1,), (1,)), ((), ()))
_TN = (((0,), (0,)), ((), ()))


def _cparams(*sem):
    return pltpu.CompilerParams(dimension_semantics=sem, vmem_limit_bytes=VMEM_LIMIT)


def _resident(shape):
    nd = len(shape)
    return pl.BlockSpec(shape, lambda *_: (0,) * nd, pipeline_mode=pl.Buffered(1))


def _silu(x):
    return x * jax.nn.sigmoid(x)


def _ffn_kernel(x_ref, g_ref, w1_ref, w3_ref, w2_ref, o_ref, h_ref, acc_ref, *, n_chunks):
    x = x_ref[...]
    r = lax.rsqrt(jnp.mean(x * x, axis=-1, keepdims=True) + NORM_EPS)
    h_ref[...] = (x * r * g_ref[...]).astype(BF16)

    for c in range(n_chunks):
        cols = slice(c * FFN_CHUNK, (c + 1) * FFN_CHUNK)
        h = h_ref[...]
        a = jnp.dot(h, w1_ref[:, cols], preferred_element_type=F32)
        b = jnp.dot(h, w3_ref[:, cols], preferred_element_type=F32)
        u = (_silu(a) * b).astype(BF16)
        d = jnp.dot(u, w2_ref[cols, :], preferred_element_type=F32)
        if c == 0:
            acc_ref[...] = d
        elif c < n_chunks - 1:
            acc_ref[...] += d
        else:
            o_ref[...] = x_ref[...] + 0.5 * (acc_ref[...] + d)


def _ffn_half(x, g, w1, w3, w2):
    n, d = x.shape
    f = w1.shape[1]
    tm = min(FFN_TM, n)
    row = pl.BlockSpec((tm, d), lambda i: (i, 0))
    return pl.pallas_call(
        functools.partial(_ffn_kernel, n_chunks=f // FFN_CHUNK),
        grid=(n // tm,),
        in_specs=[row, _resident((1, d)), _resident((d, f)), _resident((d, f)), _resident((f, d))],
        out_specs=row,
        out_shape=jax.ShapeDtypeStruct((n, d), F32),
        scratch_shapes=[pltpu.VMEM((tm, d), BF16), pltpu.VMEM((tm, d), F32)],
        compiler_params=_cparams("parallel"),
        name="ffn_half",
    )(x, g, w1, w3, w2)


def _mix_kernel(x_ref, g_ref, w_ref, cosr_ref, sinr_ref, cosd_ref, sina_ref, sinb_ref, qg_ref, kg_ref,
                bd_ref, rq_o, rk_o, rv_o, rg_o, dq_o, dk_o, dkb_o, dv_o, dvb_o, gr_o, gd_o, *, d, hd):
    x = x_ref[...]
    r = lax.rsqrt(jnp.mean(x * x, axis=-1, keepdims=True) + NORM_EPS)
    h = (x * r * g_ref[...]).astype(BF16)

    def proj(c0, width):
        return jnp.dot(h, w_ref[:, c0:c0 + width], preferred_element_type=F32)

    rw = d // 2
    cosr, sinr = cosr_ref[...], sinr_ref[...]
    for o_ref, c0, scale in ((rq_o, 0, None), (rk_o, rw, LANES ** -0.5)):
        z = proj(c0, rw)
        for hh in range(rw // LANES):
            zh = z[:, hh * LANES:(hh + 1) * LANES]
            y = zh * cosr + pltpu.roll(zh, LANES // 2, 1) * sinr
            if scale is not None:
                y = y * scale
            o_ref[:, hh * LANES:(hh + 1) * LANES] = y.astype(o_ref.dtype)

    rv_o[...] = proj(2 * rw, d).astype(rv_o.dtype)
    rg_o[...] = proj(2 * rw + d, d).astype(rg_o.dtype)

    cosd, sina, sinb = cosd_ref[...], sina_ref[...], sinb_ref[...]
    bd = bd_ref[...]
    c_dq = 2 * rw + 2 * d
    for c0, gain_ref, is_q in ((c_dq, qg_ref, True), (c_dq + d, kg_ref, False)):
        z = proj(c0, d)
        gain = gain_ref[...]
        for cc in range(d // MXU_DIM):
            zc = z[:, cc * MXU_DIM:(cc + 1) * MXU_DIM]
            ss = jnp.dot((zc * zc).astype(BF16), bd, preferred_element_type=F32)
            yn = zc * lax.rsqrt(ss * (1.0 / hd) + NORM_EPS)
            for s2 in range(MXU_DIM // LANES):
                y = yn[:, s2 * LANES:(s2 + 1) * LANES] * gain
                y = (y * cosd + pltpu.roll(y, LANES - hd // 2, 1) * sina
                     + pltpu.roll(y, hd // 2, 1) * sinb)
                col = cc * MXU_DIM + s2 * LANES
                if is_q:
                    dq_o[:, col:col + LANES] = (y * (hd ** -0.5 * LOG2E)).astype(dq_o.dtype)
                else:
                    dk_o[:, col:col + LANES] = y
                    dkb_o[:, col:col + LANES] = y.astype(dkb_o.dtype)

    z = proj(c_dq + 2 * d, d)
    dv_o[...] = z
    dvb_o[...] = z.astype(dvb_o.dtype)
    gr_o[...] = jax.nn.sigmoid(proj(c_dq + 3 * d, d)).astype(gr_o.dtype)
    gd_o[...] = jax.nn.sigmoid(proj(c_dq + 4 * d, d)).astype(gd_o.dtype)


def _mixer_in(x, g, w_in, tabs, qg, kg, bd, act_dtype, hd):
    n, d = x.shape
    tm = min(MIX_TM, n)
    n_tab = tabs[0].shape[0] // tm
    row = lambda w: pl.BlockSpec((tm, w), lambda i: (i, 0))
    tab = pl.BlockSpec((tm, LANES), lambda i: (i % n_tab, 0))
    sds = lambda w, dt: jax.ShapeDtypeStruct((n, w), dt)
    widths = (d // 2, d // 2, d, d, d, d, d, d, d, d, d)
    dtypes = (act_dtype, act_dtype, act_dtype, act_dtype, act_dtype, F32, act_dtype, F32, act_dtype,
              act_dtype, act_dtype)
    return pl.pallas_call(
        functools.partial(_mix_kernel, d=d, hd=hd),
        grid=(n // tm,),
        in_specs=[row(d), _resident((1, d)), _resident(w_in.shape), tab, tab, tab, tab, tab,
                  _resident((1, LANES)), _resident((1, LANES)), _resident((MXU_DIM, MXU_DIM))],
        out_specs=[row(w) for w in widths],
        out_shape=[sds(w, dt) for w, dt in zip(widths, dtypes)],
        compiler_params=_cparams("parallel"),
        name="mixer_in",
    )(x, g, w_in, *tabs, qg, kg, bd)


def _ret_out(o, gate, norm_g):
    r = lax.rsqrt(jnp.mean(o * o, axis=-1, keepdims=True) + NORM_EPS)
    return _silu(gate) * (o * r * norm_g)


def _ret_kernel(q_ref, k_ref, v_ref, g_ref, dm_ref, qd_ref, kd_ref, rn_ref, a_ref, s_ref, *, cdec, dk, dv):
    @pl.when(pl.program_id(1) == 0)
    def _():
        s_ref[...] = jnp.zeros_like(s_ref)

    chunk = dm_ref.shape[1]
    for c0 in range(0, q_ref.shape[0], chunk):
        rs = slice(c0, c0 + chunk)
        for h in range(N_RET_HEADS):
            ks = slice(h * dk, (h + 1) * dk)
            vs = slice(h * dv, (h + 1) * dv)
            q, k, v = q_ref[rs, ks], k_ref[rs, ks], v_ref[rs, vs]
            att = lax.dot_general(q, k, _NT, preferred_element_type=F32) * dm_ref[h]
            inner = jnp.dot(att.astype(BF16), v, preferred_element_type=F32)
            s = s_ref[0, h]
            qdec = (q.astype(F32) * qd_ref[:, ks]).astype(BF16)
            cross = jnp.dot(qdec, s.astype(BF16), preferred_element_type=F32)
            kdec = (k.astype(F32) * kd_ref[:, ks]).astype(BF16)
            s_ref[0, h] = s * cdec[h] + lax.dot_general(kdec, v, _TN, preferred_element_type=F32)
            a_ref[rs, vs] = _ret_out(inner + cross, g_ref[rs, vs].astype(F32),
                                     rn_ref[:, vs]).astype(a_ref.dtype)


def _ret_tables(chunk, dk):
    h = np.arange(N_RET_HEADS, dtype=np.float64)
    lg = np.log1p(-(2.0 ** (-5.0 - h)))
    idx = np.arange(chunk, dtype=np.float64)
    diff = idx[:, None] - idx[None, :]
    dmask = np.where(diff >= 0, np.exp(np.maximum(diff, 0.0)[None] * lg[:, None, None]), 0.0)
    q_dec = np.exp((idx[:, None] + 1.0) * lg[None, :])
    k_dec = np.exp((chunk - 1.0 - idx)[:, None] * lg[None, :])
    c_dec = tuple(float(c) for c in np.exp(chunk * lg))
    rep = lambda t: np.repeat(t, dk, axis=1)
    return dmask.astype(np.float32), rep(q_dec).astype(np.float32), rep(k_dec).astype(np.float32), c_dec


def _retention_prompt(rq, rk, rv, rg, ret_norm, batch, seq):
    n, hk = rq.shape
    hv = rv.shape[1]
    dk, dv = hk // N_RET_HEADS, hv // N_RET_HEADS
    chunk = RET_CHUNK if seq % RET_CHUNK == 0 else seq
    nc = seq // chunk
    dmask, q_dec, k_dec, c_dec = _ret_tables(chunk, dk)
    per_step = math.gcd(RET_STEP_CHUNKS, nc)
    ns = nc // per_step
    row = lambda w: pl.BlockSpec((per_step * chunk, w), lambda b, c: (b * ns + c, 0))
    return pl.pallas_call(
        functools.partial(_ret_kernel, cdec=c_dec, dk=dk, dv=dv),
        grid=(batch, ns),
        in_specs=[row(hk), row(hk), row(hv), row(hv), _resident(dmask.shape), _resident(q_dec.shape),
                  _resident(k_dec.shape), _resident((1, hv))],
        out_specs=[row(hv), pl.BlockSpec((1, N_RET_HEADS, dk, dv), lambda b, c: (b, 0, 0, 0))],
        out_shape=[jax.ShapeDtypeStruct((n, hv), BF16),
                   jax.ShapeDtypeStruct((batch, N_RET_HEADS, dk, dv), F32)],
        compiler_params=_cparams("parallel", "arbitrary"),
        name="retention_prompt",
    )(rq, rk, rv, rg, jnp.asarray(dmask), jnp.asarray(q_dec), jnp.asarray(k_dec), ret_norm)


def _ret_dec_kernel(q_ref, k_ref, v_ref, g_ref, s_ref, dm_ref, qd_ref, kd_ref, rn_ref, a_ref, so_ref,
                    cross_ref, *, cdec, dk, dv, t_len):
    rows = q_ref.shape[0]
    nb = rows // t_len
    row_seq = lax.broadcasted_iota(jnp.int32, (rows, dk), 0) // t_len

    def body(b, carry):
        r0 = pl.multiple_of(b * t_len, t_len)
        for h in range(N_RET_HEADS):
            ks = slice(h * dk, (h + 1) * dk)
            vs = slice(h * dv, (h + 1) * dv)
            s = s_ref[b, h]
            qb = q_ref[pl.ds(r0, t_len), ks] * qd_ref[pl.ds(r0, t_len), ks]
            cross_ref[pl.ds(r0, t_len), vs] = jnp.dot(qb, s, preferred_element_type=F32)
            kb = jnp.where(row_seq == b, k_ref[:, ks] * kd_ref[:, ks], 0.0)
            so_ref[b, h] = s * cdec[h] + lax.dot_general(kb, v_ref[:, vs], _TN, preferred_element_type=F32)
        return carry

    lax.fori_loop(0, nb, body, 0)

    for h in range(N_RET_HEADS):
        ks = slice(h * dk, (h + 1) * dk)
        vs = slice(h * dv, (h + 1) * dv)
        att = lax.dot_general(q_ref[:, ks], k_ref[:, ks], _NT, preferred_element_type=F32) * dm_ref[h]
        inner = jnp.dot(att, v_ref[:, vs], preferred_element_type=F32)
        a_ref[:, vs] = _ret_out(inner + cross_ref[:, vs], g_ref[:, vs], rn_ref[:, vs]).astype(a_ref.dtype)


def _retention_sample(rq, rk, rv, rg, state, ret_norm, t_len, layer):
    n, hk = rq.shape
    hv = rv.shape[1]
    dk, dv = hk // N_RET_HEADS, hv // N_RET_HEADS
    nseq = n // t_len
    bb = min(DEC_BLOCK, nseq)
    rows = bb * t_len
    dmask, q_dec, k_dec, c_dec = _ret_tables(t_len, dk)
    same_seq = np.kron(np.eye(bb, dtype=np.float32), np.ones((t_len, t_len), np.float32))
    dmask = np.tile(dmask, (1, bb, bb)) * same_seq[None]
    q_dec, k_dec = np.tile(q_dec, (bb, 1)), np.tile(k_dec, (bb, 1))
    row = lambda w: pl.BlockSpec((rows, w), lambda i: (i, 0))
    nblk = nseq // bb
    st_shape = (bb, N_RET_HEADS, dk, dv)
    st_in = pl.BlockSpec(st_shape, lambda i: (layer * nblk + i, 0, 0, 0))
    st_out = pl.BlockSpec(st_shape, lambda i: (i, 0, 0, 0))
    state = state.reshape((-1,) + state.shape[2:])
    return pl.pallas_call(
        functools.partial(_ret_dec_kernel, cdec=c_dec, dk=dk, dv=dv, t_len=t_len),
        grid=(nblk,),
        in_specs=[row(hk), row(hk), row(hv), row(hv), st_in, _resident(dmask.shape), _resident(q_dec.shape),
                  _resident(k_dec.shape), _resident((1, hv))],
        out_specs=[row(hv), st_out],
        out_shape=[jax.ShapeDtypeStruct((n, hv), F32),
                   jax.ShapeDtypeStruct((nseq, N_RET_HEADS, dk, dv), F32)],
        scratch_shapes=[pltpu.VMEM((rows, hv), F32)],
        compiler_params=_cparams("parallel"),
        name="retention_sample",
    )(rq, rk, rv, rg, state, jnp.asarray(dmask), jnp.asarray(q_dec), jnp.asarray(k_dec), ret_norm)


def _lambda(lq1_ref, lk1_ref, lq2_ref, lk2_ref, lam_init):
    s1 = jnp.sum(lq1_ref[...] * lk1_ref[...], axis=-1, keepdims=True)
    s2 = jnp.sum(lq2_ref[...] * lk2_ref[...], axis=-1, keepdims=True)
    return jnp.exp(s1) - jnp.exp(s2) + lam_init


def _diff_out(o, norm_g, lam_init):
    r = lax.rsqrt(jnp.mean(o * o, axis=-1, keepdims=True) + NORM_EPS)
    return o * r * norm_g * (1.0 - lam_init)


def _flash_kernel(q_ref, k_ref, v_ref, lq1_ref, lk1_ref, lq2_ref, lk2_ref, dn_ref, o_ref, qs_ref, vx_ref,
                  s_ref, mx_ref, mb_ref, acc_ref, *, tq, hd, lam_init):
    dv = v_ref.shape[1]
    n_lane = tq // LANES
    groups = tuple(slice(g * tq, (g + 1) * tq) for g in range(4))
    first, second, both = (0, 1), (2, 3), (0, 1, 2, 3)

    vx_ref[:, 0:dv] = v_ref[...]
    vx_ref[:, dv:2 * dv] = jnp.ones(v_ref.shape, BF16)
    lam = _lambda(lq1_ref, lk1_ref, lq2_ref, lk2_ref, lam_init)

    def tile_pair(u, tile_carry):
        lane = lax.broadcasted_iota(jnp.int32, (tq, q_ref.shape[1]), 1)
        for t in range(2):
            q = q_ref[pl.ds(pl.multiple_of((2 * u + t) * tq, tq), tq), :]
            zero = jnp.zeros_like(q)
            qs_ref[groups[2 * t], :] = jnp.where(lane < hd, q, zero)
            qs_ref[groups[2 * t + 1], :] = jnp.where(lane >= hd, q, zero)
        mx_ref[...] = jnp.full_like(mx_ref, NEG_INF)
        acc_ref[...] = jnp.zeros_like(acc_ref)

        def score_block(j, which, masked=()):
            kb = k_ref[pl.ds(pl.multiple_of(j * tq, tq), tq), :]
            for g in which:
                rows = groups[g]
                s = lax.dot_general(qs_ref[rows, :], kb, _NT, preferred_element_type=F32)
                if g in masked:
                    tri = (lax.broadcasted_iota(jnp.int32, s.shape, 1)
                           <= lax.broadcasted_iota(jnp.int32, s.shape, 0))
                    s = jnp.where(tri, s, NEG_INF)
                s_ref[j, rows, :] = s
                part = s[:, 0:LANES]
                for c in range(1, n_lane):
                    part = jnp.maximum(part, s[:, c * LANES:(c + 1) * LANES])
                mx_ref[rows, :] = jnp.maximum(mx_ref[rows, :], part)

        def score_pair(t, carry):
            score_block(2 * t, both)
            score_block(2 * t + 1, both)
            return carry

        lax.fori_loop(0, u, score_pair, 0)
        score_block(2 * u, both, masked=first)
        score_block(2 * u + 1, second, masked=second)

        mb_ref[...] = jnp.broadcast_to(jnp.max(mx_ref[...], axis=-1, keepdims=True), mb_ref.shape)

        def pv_blocks(js, which):
            for g in which:
                rows = groups[g]
                mb = mb_ref[rows, :]
                tot = None
                for j in js:
                    p = jnp.concatenate(
                        [jnp.exp2(s_ref[j, rows, c * LANES:(c + 1) * LANES] - mb) for c in range(n_lane)],
                        axis=1).astype(BF16)
                    vb = vx_ref[pl.ds(pl.multiple_of(j * tq, tq), tq), :]
                    d = jnp.dot(p, vb, preferred_element_type=F32)
                    tot = d if tot is None else tot + d
                acc_ref[rows, :] += tot

        def pv_pair(t, carry):
            pv_blocks((2 * t, 2 * t + 1), both)
            return carry

        lax.fori_loop(0, u, pv_pair, 0)
        pv_blocks((2 * u,), first)
        pv_blocks((2 * u, 2 * u + 1), second)

        for t in range(2):
            o1, o2 = (acc_ref[groups[2 * t + m], 0:dv] / acc_ref[groups[2 * t + m], dv:2 * dv] for m in range(2))
            out_rows = pl.ds(pl.multiple_of((2 * u + t) * tq, tq), tq)
            o_ref[out_rows, :] = _diff_out(o1 - lam * o2, dn_ref[...], lam_init).astype(o_ref.dtype)
        return tile_carry

    lax.fori_loop(0, q_ref.shape[0] // (2 * tq), tile_pair, 0)


def _diff_attn_prompt(dq, dk, dv, lam_params, diff_norm, batch, seq, lam_init):
    n, hw = dq.shape
    hd = hw // (2 * N_DIFF_HEADS)
    w = 2 * hd
    tq = min(FLASH_TQ, seq // 2)
    nq = seq // tq
    assert seq % (2 * tq) == 0 and tq % LANES == 0
    kv = pl.BlockSpec((seq, w), lambda b, h: (b, h))
    return pl.pallas_call(
        functools.partial(_flash_kernel, tq=tq, hd=hd, lam_init=lam_init),
        grid=(batch, N_DIFF_HEADS),
        in_specs=[kv, kv, kv] + [_resident((1, hd))] * 4 + [_resident((1, w))],
        out_specs=kv,
        out_shape=jax.ShapeDtypeStruct((n, hw), BF16),
        scratch_shapes=[pltpu.VMEM((4 * tq, w), BF16), pltpu.VMEM((seq, 2 * w), BF16),
                        pltpu.VMEM((nq, 4 * tq, tq), F32), pltpu.VMEM((4 * tq, LANES), F32),
                        pltpu.VMEM((4 * tq, LANES), F32), pltpu.VMEM((4 * tq, 2 * w), F32)],
        compiler_params=_cparams("parallel", "parallel"),
        name="diff_attn_prompt",
    )(dq, dk, dv, *lam_params, diff_norm)


def _paged_kernel(pt_ref, q_ref, kn_ref, vn_ref, lq1_ref, lk1_ref, lq2_ref, lk2_ref, dn_ref, *rest,
                  n_pp, t_len, hd, lam_init):
    k_refs, v_refs = rest[:n_pp], rest[n_pp:2 * n_pp]
    o_ref, qbd_ref, m_ref, l_ref, acc_ref = rest[2 * n_pp:]
    p = pl.program_id(1)
    n_rows, width = qbd_ref.shape

    @pl.when(p == 0)
    def _():
        q = jnp.tile(q_ref[...], (n_rows // t_len, 1))
        r = lax.broadcasted_iota(jnp.int32, q.shape, 0) // t_len
        c = lax.broadcasted_iota(jnp.int32, q.shape, 1) // hd
        qbd_ref[...] = jnp.where(r == c, q, 0.0).astype(BF16)
        m_ref[...] = jnp.full_like(m_ref, NEG_INF)
        l_ref[...] = jnp.zeros_like(l_ref)
        acc_ref[...] = jnp.zeros_like(acc_ref)

    qbd = qbd_ref[...]

    def update(s, v):
        m_prev = m_ref[...]
        m_new = jnp.maximum(m_prev, jnp.max(s, axis=-1, keepdims=True))
        corr = jnp.exp2(m_prev - m_new)
        pe = jnp.exp2(s - m_new)
        l_ref[...] = l_ref[...] * corr + jnp.sum(pe, axis=-1, keepdims=True)
        acc_ref[...] = acc_ref[...] * corr + jnp.dot(pe.astype(BF16), v, preferred_element_type=F32)
        m_ref[...] = m_new

    def page_rows(ref):
        page = ref.shape[0] // N_DIFF_HEADS
        return jnp.concatenate([ref[pl.ds(h, page, stride=N_DIFF_HEADS), :] for h in range(N_DIFF_HEADS)],
                               axis=1).astype(BF16)

    k_all = jnp.concatenate([page_rows(k_refs[r]) for r in range(n_pp)], axis=0)
    v_all = jnp.concatenate([page_rows(v_refs[r]) for r in range(n_pp)], axis=0)
    update(lax.dot_general(qbd, k_all, _NT, preferred_element_type=F32), v_all)

    @pl.when(p == pl.num_programs(1) - 1)
    def _():
        pad = jnp.zeros((LANES - t_len, width), F32)
        kn = jnp.concatenate([kn_ref[...], pad], axis=0).astype(BF16)
        vn = jnp.concatenate([vn_ref[...], pad], axis=0).astype(BF16)
        sn = lax.dot_general(qbd, kn, _NT, preferred_element_type=F32)
        qpos = lax.broadcasted_iota(jnp.int32, sn.shape, 0) % t_len
        kpos = lax.broadcasted_iota(jnp.int32, sn.shape, 1)
        sn = jnp.where(kpos <= qpos, sn, NEG_INF)
        update(sn, vn)
        o = acc_ref[...] / l_ref[...]
        lam = _lambda(lq1_ref, lk1_ref, lq2_ref, lk2_ref, lam_init)
        dvw = width // N_DIFF_HEADS
        for h in range(N_DIFF_HEADS):
            cs = slice(h * dvw, (h + 1) * dvw)
            o1 = o[(2 * h) * t_len:(2 * h + 1) * t_len, cs]
            o2 = o[(2 * h + 1) * t_len:(2 * h + 2) * t_len, cs]
            o_ref[:, cs] = _diff_out(o1 - lam * o2, dn_ref[...], lam_init).astype(o_ref.dtype)


def _diff_attn_sample(dq, dk, dv, cache_k, cache_v, page_table, lam_params, diff_norm, t_len, lam_init, layer):
    n, hw = dq.shape
    hd = hw // (2 * N_DIFF_HEADS)
    nseq, n_pages = page_table.shape
    depth, n_pool, page = cache_k.shape[:3]
    page_rows, w = page * N_DIFF_HEADS, hw // N_DIFF_HEADS
    ck = cache_k.reshape(depth * n_pool, page_rows, w)
    cv = cache_v.reshape(depth * n_pool, page_rows, w)
    page_table = page_table + layer * n_pool
    n_pp = math.gcd(PAGES_PER_STEP, n_pages)
    n_rows = 2 * N_DIFF_HEADS * t_len
    row = pl.BlockSpec((t_len, hw), lambda b, p, pt: (b, 0))
    const = lambda shape: pl.BlockSpec(shape, lambda b, p, pt: (0, 0))
    page_spec = lambda r: pl.BlockSpec((None, page_rows, w), lambda b, p, pt: (pt[b, p * n_pp + r], 0, 0))
    grid_spec = pltpu.PrefetchScalarGridSpec(
        num_scalar_prefetch=1,
        grid=(nseq, n_pages // n_pp),
        in_specs=[row, row, row] + [const((1, hd))] * 4 + [const((1, 2 * hd))]
        + [page_spec(r) for r in range(n_pp)] * 2,
        out_specs=row,
        scratch_shapes=[pltpu.VMEM((n_rows, hw), BF16), pltpu.VMEM((n_rows, 1), F32),
                        pltpu.VMEM((n_rows, 1), F32), pltpu.VMEM((n_rows, hw), F32)],
    )
    return pl.pallas_call(
        functools.partial(_paged_kernel, n_pp=n_pp, t_len=t_len, hd=hd, lam_init=lam_init),
        grid_spec=grid_spec,
        out_shape=jax.ShapeDtypeStruct((n, hw), F32),
        compiler_params=_cparams("parallel", "arbitrary"),
        name="diff_attn_sample",
    )(page_table, dq, dk, dv, *lam_params, diff_norm, *([ck] * n_pp), *([cv] * n_pp))


def _merge_kernel(x_ref, ar_ref, ad_ref, gr_ref, gd_ref, wr_ref, wd_ref, wo_ref, o_ref):
    y_ret = jnp.dot(ar_ref[...].astype(BF16), wr_ref[...], preferred_element_type=F32)
    y_diff = jnp.dot(ad_ref[...].astype(BF16), wd_ref[...], preferred_element_type=F32)
    merged = gr_ref[...].astype(F32) * y_ret + gd_ref[...].astype(F32) * y_diff
    o_ref[...] = x_ref[...] + jnp.dot(merged.astype(BF16), wo_ref[...], preferred_element_type=F32)


def _merge(x, a_ret, a_diff, gr, gd, wr, wd, wo):
    n, d = x.shape
    tm = min(MERGE_TM, n)
    row = pl.BlockSpec((tm, d), lambda i: (i, 0))
    return pl.pallas_call(
        _merge_kernel,
        grid=(n // tm,),
        in_specs=[row] * 5 + [_resident(wr.shape), _resident(wd.shape), _resident(wo.shape)],
        out_specs=row,
        out_shape=jax.ShapeDtypeStruct((n, d), F32),
        compiler_params=_cparams("parallel"),
        name="merge",
    )(x, a_ret, a_diff, gr, gd, wr, wd, wo)


def _rotary_tables(pos, dk, hd):
    def cs(half):
        inv = ROPE_THETA ** (-jnp.arange(half, dtype=F32) / half)
        ang = pos[:, None] * inv[None, :]
        return jnp.cos(ang), jnp.sin(ang)

    cr, sr = cs(dk // 2)
    cd, sd = cs(hd // 2)
    zd = jnp.zeros_like(sd)
    reps = LANES // hd
    return (jnp.concatenate([cr, cr], axis=1), jnp.concatenate([-sr, sr], axis=1),
            jnp.tile(cd, (1, 2 * reps)), jnp.tile(jnp.concatenate([-sd, zd], axis=1), (1, reps)),
            jnp.tile(jnp.concatenate([zd, sd], axis=1), (1, reps)))


def kernel(x_prompt, x_sample, cache_k, cache_v, state_ret, page_table, norm_ffn1, ffn1_w1, ffn1_w3, ffn1_w2,
           norm_mix, w_in, ret_norm, w_ret_proj, q_norm, k_norm, lambda_q1, lambda_k1, lambda_q2, lambda_k2,
           diff_norm, w_diff_proj, w_o, norm_ffn2, ffn2_w1, ffn2_w3, ffn2_w2):
    B, S, D = x_prompt.shape
    DB, T, _ = x_sample.shape
    depth = w_in.shape[0]
    hd = D // (2 * N_DIFF_HEADS)
    dk = D // (2 * N_RET_HEADS)
    assert dk == LANES and 2 * hd == LANES, "head layouts assume 128-lane retention heads and 64-lane diff heads"
    assert D % MXU_DIM == 0 and ffn1_w1.shape[-1] % FFN_CHUNK == 0
    past_len = page_table.shape[1] * cache_k.shape[2]

    tabs_p = _rotary_tables(jnp.arange(S, dtype=F32), dk, hd)
    mix_rows_s = min(MIX_TM, DB * T)
    tabs_s = tuple(jnp.tile(t, (mix_rows_s // T, 1))
                   for t in _rotary_tables(past_len + jnp.arange(T, dtype=F32), dk, hd))
    blk = np.arange(MXU_DIM) // hd
    bd = jnp.asarray(blk[:, None] == blk[None, :], dtype=BF16)

    xp = x_prompt.reshape(B * S, D)
    xs = x_sample.reshape(DB * T, D)
    outs = [[] for _ in range(6)]
    for l in range(depth):
        lam_init = 0.8 - 0.6 * math.exp(-0.3 * l)
        row = lambda a: a[l].reshape(1, -1)
        ffn1 = (row(norm_ffn1), ffn1_w1[l].astype(BF16), ffn1_w3[l].astype(BF16), ffn1_w2[l].astype(BF16))
        ffn2 = (row(norm_ffn2), ffn2_w1[l].astype(BF16), ffn2_w3[l].astype(BF16), ffn2_w2[l].astype(BF16))
        w_in_b = w_in[l].astype(BF16)
        qg = jnp.tile(row(q_norm), (1, LANES // hd))
        kg = jnp.tile(row(k_norm), (1, LANES // hd))
        lam_params = (row(lambda_q1), row(lambda_k1), row(lambda_q2), row(lambda_k2))
        rn, dn = row(ret_norm), row(diff_norm)
        proj = (w_ret_proj[l].astype(BF16), w_diff_proj[l].astype(BF16), w_o[l].astype(BF16))

        x1 = _ffn_half(xp, *ffn1)
        rq, rk, rv, rg, dq, dk32, dkb, dv32, dvb, gr, gd = _mixer_in(
            x1, row(norm_mix), w_in_b, tabs_p, qg, kg, bd, BF16, hd)
        a_ret, s_p = _retention_prompt(rq, rk, rv, rg, rn, B, S)
        a_diff = _diff_attn_prompt(dq, dkb, dvb, lam_params, dn, B, S, lam_init)
        xp = _ffn_half(_merge(x1, a_ret, a_diff, gr, gd, *proj), *ffn2)
        outs[0].append(dk32.reshape(B, S, N_DIFF_HEADS, 2 * hd))
        outs[1].append(dv32.reshape(B, S, N_DIFF_HEADS, 2 * hd))
        outs[2].append(s_p)

        x1 = _ffn_half(xs, *ffn1)
        rq, rk, rv, rg, dq, dk32, _, dv32, _, gr, gd = _mixer_in(
            x1, row(norm_mix), w_in_b, tabs_s, qg, kg, bd, F32, hd)
        a_ret, s_s = _retention_sample(rq, rk, rv, rg, state_ret, rn, T, l)
        a_diff = _diff_attn_sample(dq, dk32, dv32, cache_k, cache_v, page_table, lam_params, dn, T,
                                   lam_init, l)
        xs = _ffn_half(_merge(x1, a_ret, a_diff, gr, gd, *proj), *ffn2)
        outs[3].append(dk32.reshape(DB, T, N_DIFF_HEADS, 2 * hd))
        outs[4].append(dv32.reshape(DB, T, N_DIFF_HEADS, 2 * hd))
        outs[5].append(s_s)

    kp, vp, sp, ks, vs, ss = (jnp.stack(o) for o in outs)
    return xp.reshape(B, S, D), xs.reshape(DB, T, D), kp, vp, sp, ks, vs, ss
```

```python
import functools
import math

import numpy as np
import jax
import jax.numpy as jnp
from jax import lax
from jax.experimental import pallas as pl
from jax.experimental.pallas import tpu as pltpu

F32 = jnp.float32
BF16 = jnp.bfloat16

N_RET_HEADS = 4
N_DIFF_HEADS = 8
RET_CHUNK = 128
RET_STEP_CHUNKS = 8
ROPE_THETA = 10000.0
NORM_EPS = 1e-6
NEG_INF = -1e30

LANES = 128
MXU_DIM = 256
VMEM_LIMIT = 56 * 1024 * 1024

FFN_TM = 512
FFN_CHUNK = MXU_DIM
MIX_TM = 256
MERGE_TM = 512
FLASH_TQ = 512
PAGES_PER_STEP = 16
LOG2E = math.log2(math.e)
DEC_BLOCK = 16

_NT = (((1,), (1,)), ((), ()))
_TN = (((0,), (0,)), ((), ()))


def _cparams(*sem):
    return pltpu.CompilerParams(dimension_semantics=sem, vmem_limit_bytes=VMEM_LIMIT)


def _resident(shape):
    nd = len(shape)
    return pl.BlockSpec(shape, lambda *_: (0,) * nd, pipeline_mode=pl.Buffered(1))


def _silu(x):
    return x * jax.nn.sigmoid(x)


def _ffn_kernel(x_ref, g_ref, w1_ref, w3_ref, w2_ref, o_ref, h_ref, acc_ref, *, n_chunks):
    x = x_ref[...]
    r = lax.rsqrt(jnp.mean(x * x, axis=-1, keepdims=True) + NORM_EPS)
    h_ref[...] = (x * r * g_ref[...]).astype(BF16)

    for c in range(n_chunks):
        cols = slice(c * FFN_CHUNK, (c + 1) * FFN_CHUNK)
        h = h_ref[...]
        a = jnp.dot(h, w1_ref[:, cols], preferred_element_type=F32)
        b = jnp.dot(h, w3_ref[:, cols], preferred_element_type=F32)
        u = (_silu(a) * b).astype(BF16)
        d = jnp.dot(u, w2_ref[cols, :], preferred_element_type=F32)
        if c == 0:
            acc_ref[...] = d
        elif c < n_chunks - 1:
            acc_ref[...] += d
        else:
            o_ref[...] = x_ref[...] + 0.5 * (acc_ref[...] + d)


def _ffn_half(x, g, w1, w3, w2):
    n, d = x.shape
    f = w1.shape[1]
    tm = min(FFN_TM, n)
    row = pl.BlockSpec((tm, d), lambda i: (i, 0))
    return pl.pallas_call(
        functools.partial(_ffn_kernel, n_chunks=f // FFN_CHUNK),
        grid=(n // tm,),
        in_specs=[row, _resident((1, d)), _resident((d, f)), _resident((d, f)), _resident((f, d))],
        out_specs=row,
        out_shape=jax.ShapeDtypeStruct((n, d), F32),
        scratch_shapes=[pltpu.VMEM((tm, d), BF16), pltpu.VMEM((tm, d), F32)],
        compiler_params=_cparams("parallel"),
        name="ffn_half",
    )(x, g, w1, w3, w2)


def _mix_kernel(x_ref, g_ref, w_ref, cosr_ref, sinr_ref, cosd_ref, sina_ref, sinb_ref, qg_ref, kg_ref,
                bd_ref, rq_o, rk_o, rv_o, rg_o, dq_o, dk_o, dkb_o, dv_o, dvb_o, gr_o, gd_o, *, d, hd):
    x = x_ref[...]
    r = lax.rsqrt(jnp.mean(x * x, axis=-1, keepdims=True) + NORM_EPS)
    h = (x * r * g_ref[...]).astype(BF16)

    def proj(c0, width):
        return jnp.dot(h, w_ref[:, c0:c0 + width], preferred_element_type=F32)

    rw = d // 2
    cosr, sinr = cosr_ref[...], sinr_ref[...]
    for o_ref, c0, scale in ((rq_o, 0, None), (rk_o, rw, LANES ** -0.5)):
        z = proj(c0, rw)
        for hh in range(rw // LANES):
            zh = z[:, hh * LANES:(hh + 1) * LANES]
            y = zh * cosr + pltpu.roll(zh, LANES // 2, 1) * sinr
            if scale is not None:
                y = y * scale
            o_ref[:, hh * LANES:(hh + 1) * LANES] = y.astype(o_ref.dtype)

    rv_o[...] = proj(2 * rw, d).astype(rv_o.dtype)
    rg_o[...] = proj(2 * rw + d, d).astype(rg_o.dtype)

    cosd, sina, sinb = cosd_ref[...], sina_ref[...], sinb_ref[...]
    bd = bd_ref[...]
    c_dq = 2 * rw + 2 * d
    for c0, gain_ref, is_q in ((c_dq, qg_ref, True), (c_dq + d, kg_ref, False)):
        z = proj(c0, d)
        gain = gain_ref[...]
        for cc in range(d // MXU_DIM):
            zc = z[:, cc * MXU_DIM:(cc + 1) * MXU_DIM]
            ss = jnp.dot((zc * zc).astype(BF16), bd, preferred_element_type=F32)
            yn = zc * lax.rsqrt(ss * (1.0 / hd) + NORM_EPS)
            for s2 in range(MXU_DIM // LANES):
                y = yn[:, s2 * LANES:(s2 + 1) * LANES] * gain
                y = (y * cosd + pltpu.roll(y, LANES - hd // 2, 1) * sina
                     + pltpu.roll(y, hd // 2, 1) * sinb)
                col = cc * MXU_DIM + s2 * LANES
                if is_q:
                    dq_o[:, col:col + LANES] = (y * (hd ** -0.5 * LOG2E)).astype(dq_o.dtype)
                else:
                    dk_o[:, col:col + LANES] = y
                    dkb_o[:, col:col + LANES] = y.astype(dkb_o.dtype)

    z = proj(c_dq + 2 * d, d)
    dv_o[...] = z
    dvb_o[...] = z.astype(dvb_o.dtype)
    gr_o[...] = jax.nn.sigmoid(proj(c_dq + 3 * d, d)).astype(gr_o.dtype)
    gd_o[...] = jax.nn.sigmoid(proj(c_dq + 4 * d, d)).astype(gd_o.dtype)


def _mixer_in(x, g, w_in, tabs, qg, kg, bd, act_dtype, hd):
    n, d = x.shape
    tm = min(MIX_TM, n)
    n_tab = tabs[0].shape[0] // tm
    row = lambda w: pl.BlockSpec((tm, w), lambda i: (i, 0))
    tab = pl.BlockSpec((tm, LANES), lambda i: (i % n_tab, 0))
    sds = lambda w, dt: jax.ShapeDtypeStruct((n, w), dt)
    widths = (d // 2, d // 2, d, d, d, d, d, d, d, d, d)
    dtypes = (act_dtype, act_dtype, act_dtype, act_dtype, act_dtype, F32, act_dtype, F32, act_dtype,
              act_dtype, act_dtype)
    return pl.pallas_call(
        functools.partial(_mix_kernel, d=d, hd=hd),
        grid=(n // tm,),
        in_specs=[row(d), _resident((1, d)), _resident(w_in.shape), tab, tab, tab, tab, tab,
                  _resident((1, LANES)), _resident((1, LANES)), _resident((MXU_DIM, MXU_DIM))],
        out_specs=[row(w) for w in widths],
        out_shape=[sds(w, dt) for w, dt in zip(widths, dtypes)],
        compiler_params=_cparams("parallel"),
        name="mixer_in",
    )(x, g, w_in, *tabs, qg, kg, bd)


def _ret_out(o, gate, norm_g):
    r = lax.rsqrt(jnp.mean(o * o, axis=-1, keepdims=True) + NORM_EPS)
    return _silu(gate) * (o * r * norm_g)


def _ret_kernel(q_ref, k_ref, v_ref, g_ref, dm_ref, qd_ref, kd_ref, rn_ref, a_ref, s_ref, *, cdec, dk, dv):
    @pl.when(pl.program_id(1) == 0)
    def _():
        s_ref[...] = jnp.zeros_like(s_ref)

    chunk = dm_ref.shape[1]
    for c0 in range(0, q_ref.shape[0], chunk):
        rs = slice(c0, c0 + chunk)
        for h in range(N_RET_HEADS):
            ks = slice(h * dk, (h + 1) * dk)
            vs = slice(h * dv, (h + 1) * dv)
            q, k, v = q_ref[rs, ks], k_ref[rs, ks], v_ref[rs, vs]
            att = lax.dot_general(q, k, _NT, preferred_element_type=F32) * dm_ref[h]
            inner = jnp.dot(att.astype(BF16), v, preferred_element_type=F32)
            s = s_ref[0, h]
            qdec = (q.astype(F32) * qd_ref[:, ks]).astype(BF16)
            cross = jnp.dot(qdec, s.astype(BF16), preferred_element_type=F32)
            kdec = (k.astype(F32) * kd_ref[:, ks]).astype(BF16)
            s_ref[0, h] = s * cdec[h] + lax.dot_general(kdec, v, _TN, preferred_element_type=F32)
            a_ref[rs, vs] = _ret_out(inner + cross, g_ref[rs, vs].astype(F32),
                                     rn_ref[:, vs]).astype(a_ref.dtype)


def _ret_tables(chunk, dk):
    h = np.arange(N_RET_HEADS, dtype=np.float64)
    lg = np.log1p(-(2.0 ** (-5.0 - h)))
    idx = np.arange(chunk, dtype=np.float64)
    diff = idx[:, None] - idx[None, :]
    dmask = np.where(diff >= 0, np.exp(np.maximum(diff, 0.0)[None] * lg[:, None, None]), 0.0)
    q_dec = np.exp((idx[:, None] + 1.0) * lg[None, :])
    k_dec = np.exp((chunk - 1.0 - idx)[:, None] * lg[None, :])
    c_dec = tuple(float(c) for c in np.exp(chunk * lg))
    rep = lambda t: np.repeat(t, dk, axis=1)
    return dmask.astype(np.float32), rep(q_dec).astype(np.float32), rep(k_dec).astype(np.float32), c_dec


def _retention_prompt(rq, rk, rv, rg, ret_norm, batch, seq):
    n, hk = rq.shape
    hv = rv.shape[1]
    dk, dv = hk // N_RET_HEADS, hv // N_RET_HEADS
    chunk = RET_CHUNK if seq % RET_CHUNK == 0 else seq
    nc = seq // chunk
    dmask, q_dec, k_dec, c_dec = _ret_tables(chunk, dk)
    per_step = math.gcd(RET_STEP_CHUNKS, nc)
    ns = nc // per_step
    row = lambda w: pl.BlockSpec((per_step * chunk, w), lambda b, c: (b * ns + c, 0))
    return pl.pallas_call(
        functools.partial(_ret_kernel, cdec=c_dec, dk=dk, dv=dv),
        grid=(batch, ns),
        in_specs=[row(hk), row(hk), row(hv), row(hv), _resident(dmask.shape), _resident(q_dec.shape),
                  _resident(k_dec.shape), _resident((1, hv))],
        out_specs=[row(hv), pl.BlockSpec((1, N_RET_HEADS, dk, dv), lambda b, c: (b, 0, 0, 0))],
        out_shape=[jax.ShapeDtypeStruct((n, hv), BF16),
                   jax.ShapeDtypeStruct((batch, N_RET_HEADS, dk, dv), F32)],
        compiler_params=_cparams("parallel", "arbitrary"),
        name="retention_prompt",
    )(rq, rk, rv, rg, jnp.asarray(dmask), jnp.asarray(q_dec), jnp.asarray(k_dec), ret_norm)


def _ret_dec_kernel(q_ref, k_ref, v_ref, g_ref, s_ref, dm_ref, qd_ref, kd_ref, rn_ref, a_ref, so_ref,
                    cross_ref, *, cdec, dk, dv, t_len):
    rows = q_ref.shape[0]
    nb = rows // t_len
    row_seq = lax.broadcasted_iota(jnp.int32, (rows, dk), 0) // t_len

    def body(b, carry):
        r0 = pl.multiple_of(b * t_len, t_len)
        for h in range(N_RET_HEADS):
            ks = slice(h * dk, (h + 1) * dk)
            vs = slice(h * dv, (h + 1) * dv)
            s = s_ref[b, h]
            qb = q_ref[pl.ds(r0, t_len), ks] * qd_ref[pl.ds(r0, t_len), ks]
            cross_ref[pl.ds(r0, t_len), vs] = jnp.dot(qb, s, preferred_element_type=F32)
            kb = jnp.where(row_seq == b, k_ref[:, ks] * kd_ref[:, ks], 0.0)
            so_ref[b, h] = s * cdec[h] + lax.dot_general(kb, v_ref[:, vs], _TN, preferred_element_type=F32)
        return carry

    lax.fori_loop(0, nb, body, 0)

    for h in range(N_RET_HEADS):
        ks = slice(h * dk, (h + 1) * dk)
        vs = slice(h * dv, (h + 1) * dv)
        att = lax.dot_general(q_ref[:, ks], k_ref[:, ks], _NT, preferred_element_type=F32) * dm_ref[h]
        inner = jnp.dot(att, v_ref[:, vs], preferred_element_type=F32)
        a_ref[:, vs] = _ret_out(inner + cross_ref[:, vs], g_ref[:, vs], rn_ref[:, vs]).astype(a_ref.dtype)


def _retention_sample(rq, rk, rv, rg, state, ret_norm, t_len, layer):
    n, hk = rq.shape
    hv = rv.shape[1]
    dk, dv = hk // N_RET_HEADS, hv // N_RET_HEADS
    nseq = n // t_len
    bb = min(DEC_BLOCK, nseq)
    rows = bb * t_len
    dmask, q_dec, k_dec, c_dec = _ret_tables(t_len, dk)
    same_seq = np.kron(np.eye(bb, dtype=np.float32), np.ones((t_len, t_len), np.float32))
    dmask = np.tile(dmask, (1, bb, bb)) * same_seq[None]
    q_dec, k_dec = np.tile(q_dec, (bb, 1)), np.tile(k_dec, (bb, 1))
    row = lambda w: pl.BlockSpec((rows, w), lambda i: (i, 0))
    nblk = nseq // bb
    st_shape = (bb, N_RET_HEADS, dk, dv)
    st_in = pl.BlockSpec(st_shape, lambda i: (layer * nblk + i, 0, 0, 0))
    st_out = pl.BlockSpec(st_shape, lambda i: (i, 0, 0, 0))
    state = state.reshape((-1,) + state.shape[2:])
    return pl.pallas_call(
        functools.partial(_ret_dec_kernel, cdec=c_dec, dk=dk, dv=dv, t_len=t_len),
        grid=(nblk,),
        in_specs=[row(hk), row(hk), row(hv), row(hv), st_in, _resident(dmask.shape), _resident(q_dec.shape),
                  _resident(k_dec.shape), _resident((1, hv))],
        out_specs=[row(hv), st_out],
        out_shape=[jax.ShapeDtypeStruct((n, hv), F32),
                   jax.ShapeDtypeStruct((nseq, N_RET_HEADS, dk, dv), F32)],
        scratch_shapes=[pltpu.VMEM((rows, hv), F32)],
        compiler_params=_cparams("parallel"),
        name="retention_sample",
    )(rq, rk, rv, rg, state, jnp.asarray(dmask), jnp.asarray(q_dec), jnp.asarray(k_dec), ret_norm)


def _lambda(lq1_ref, lk1_ref, lq2_ref, lk2_ref, lam_init):
    s1 = jnp.sum(lq1_ref[...] * lk1_ref[...], axis=-1, keepdims=True)
    s2 = jnp.sum(lq2_ref[...] * lk2_ref[...], axis=-1, keepdims=True)
    return jnp.exp(s1) - jnp.exp(s2) + lam_init


def _diff_out(o, norm_g, lam_init):
    r = lax.rsqrt(jnp.mean(o * o, axis=-1, keepdims=True) + NORM_EPS)
    return o * r * norm_g * (1.0 - lam_init)


def _flash_kernel(q_ref, k_ref, v_ref, lq1_ref, lk1_ref, lq2_ref, lk2_ref, dn_ref, o_ref, qs_ref, vx_ref,
                  s_ref, mx_ref, mb_ref, acc_ref, *, tq, hd, lam_init):
    dv = v_ref.shape[1]
    n_lane = tq // LANES
    groups = tuple(slice(g * tq, (g + 1) * tq) for g in range(4))
    first, second, both = (0, 1), (2, 3), (0, 1, 2, 3)

    vx_ref[:, 0:dv] = v_ref[...]
    vx_ref[:, dv:2 * dv] = jnp.ones(v_ref.shape, BF16)
    lam = _lambda(lq1_ref, lk1_ref, lq2_ref, lk2_ref, lam_init)

    def tile_pair(u, tile_carry):
        lane = lax.broadcasted_iota(jnp.int32, (tq, q_ref.shape[1]), 1)
        for t in range(2):
            q = q_ref[pl.ds(pl.multiple_of((2 * u + t) * tq, tq), tq), :]
            zero = jnp.zeros_like(q)
            qs_ref[groups[2 * t], :] = jnp.where(lane < hd, q, zero)
            qs_ref[groups[2 * t + 1], :] = jnp.where(lane >= hd, q, zero)

        def score_block(j, which, masked=(), init=()):
            kb = k_ref[pl.ds(pl.multiple_of(j * tq, tq), tq), :]
            for g in which:
                rows = groups[g]
                s = lax.dot_general(qs_ref[rows, :], kb, _NT, preferred_element_type=F32)
                if g in masked:
                    tri = (lax.broadcasted_iota(jnp.int32, s.shape, 1)
                           <= lax.broadcasted_iota(jnp.int32, s.shape, 0))
                    s = jnp.where(tri, s, NEG_INF)
                s_ref[j, rows, :] = s
                part = s[:, 0:LANES]
                for c in range(1, n_lane):
                    part = jnp.maximum(part, s[:, c * LANES:(c + 1) * LANES])
                mx_ref[rows, :] = part if g in init else jnp.maximum(mx_ref[rows, :], part)

        def score_pair(t, carry):
            score_block(2 * t, both)
            score_block(2 * t + 1, both)
            return carry

        score_block(2 * u, both, masked=first, init=both)
        score_block(2 * u + 1, second, masked=second)
        lax.fori_loop(0, u, score_pair, 0)

        mb_ref[...] = jnp.broadcast_to(jnp.max(mx_ref[...], axis=-1, keepdims=True), mb_ref.shape)

        def pv_blocks(js, which, init=False):
            for g in which:
                rows = groups[g]
                mb = mb_ref[rows, :]
                tot = None
                for j in js:
                    p = jnp.concatenate(
                        [jnp.exp2(s_ref[j, rows, c * LANES:(c + 1) * LANES] - mb) for c in range(n_lane)],
                        axis=1).astype(BF16)
                    vb = vx_ref[pl.ds(pl.multiple_of(j * tq, tq), tq), :]
                    d = jnp.dot(p, vb, preferred_element_type=F32)
                    tot = d if tot is None else tot + d
                acc_ref[rows, :] = tot if init else acc_ref[rows, :] + tot

        def pv_pair(t, carry):
            pv_blocks((2 * t, 2 * t + 1), both)
            return carry

        pv_blocks((2 * u,), first, init=True)
        pv_blocks((2 * u, 2 * u + 1), second, init=True)
        lax.fori_loop(0, u, pv_pair, 0)

        for t in range(2):
            o1, o2 = (acc_ref[groups[2 * t + m], 0:dv] / acc_ref[groups[2 * t + m], dv:2 * dv] for m in range(2))
            out_rows = pl.ds(pl.multiple_of((2 * u + t) * tq, tq), tq)
            o_ref[out_rows, :] = _diff_out(o1 - lam * o2, dn_ref[...], lam_init).astype(o_ref.dtype)
        return tile_carry

    lax.fori_loop(0, q_ref.shape[0] // (2 * tq), tile_pair, 0)


def _diff_attn_prompt(dq, dk, dv, lam_params, diff_norm, batch, seq, lam_init):
    n, hw = dq.shape
    hd = hw // (2 * N_DIFF_HEADS)
    w = 2 * hd
    tq = min(FLASH_TQ, seq // 2)
    nq = seq // tq
    assert seq % (2 * tq) == 0 and tq % LANES == 0
    kv = pl.BlockSpec((seq, w), lambda b, h: (b, h))
    return pl.pallas_call(
        functools.partial(_flash_kernel, tq=tq, hd=hd, lam_init=lam_init),
        grid=(batch, N_DIFF_HEADS),
        in_specs=[kv, kv, kv] + [_resident((1, hd))] * 4 + [_resident((1, w))],
        out_specs=kv,
        out_shape=jax.ShapeDtypeStruct((n, hw), BF16),
        scratch_shapes=[pltpu.VMEM((4 * tq, w), BF16), pltpu.VMEM((seq, 2 * w), BF16),
                        pltpu.VMEM((nq, 4 * tq, tq), F32), pltpu.VMEM((4 * tq, LANES), F32),
                        pltpu.VMEM((4 * tq, LANES), F32), pltpu.VMEM((4 * tq, 2 * w), F32)],
        compiler_params=_cparams("parallel", "parallel"),
        name="diff_attn_prompt",
    )(dq, dk, dv, *lam_params, diff_norm)


def _paged_kernel(pt_ref, q_ref, kn_ref, vn_ref, lq1_ref, lk1_ref, lq2_ref, lk2_ref, dn_ref, *rest,
                  n_pp, t_len, hd, lam_init):
    k_refs, v_refs = rest[:n_pp], rest[n_pp:2 * n_pp]
    o_ref, qbd_ref, m_ref, l_ref, acc_ref = rest[2 * n_pp:]
    p = pl.program_id(1)
    n_rows, width = qbd_ref.shape

    @pl.when(p == 0)
    def _():
        q = jnp.tile(q_ref[...], (n_rows // t_len, 1))
        r = lax.broadcasted_iota(jnp.int32, q.shape, 0) // t_len
        c = lax.broadcasted_iota(jnp.int32, q.shape, 1) // hd
        qbd_ref[...] = jnp.where(r == c, q, 0.0).astype(BF16)
        m_ref[...] = jnp.full_like(m_ref, NEG_INF)
        l_ref[...] = jnp.zeros_like(l_ref)
        acc_ref[...] = jnp.zeros_like(acc_ref)

    qbd = qbd_ref[...]

    def update(s, v):
        m_prev = m_ref[...]
        m_new = jnp.maximum(m_prev, jnp.max(s, axis=-1, keepdims=True))
        corr = jnp.exp2(m_prev - m_new)
        pe = jnp.exp2(s - m_new)
        l_ref[...] = l_ref[...] * corr + jnp.sum(pe, axis=-1, keepdims=True)
        acc_ref[...] = acc_ref[...] * corr + jnp.dot(pe.astype(BF16), v, preferred_element_type=F32)
        m_ref[...] = m_new

    def page_rows(ref):
        page = ref.shape[0] // N_DIFF_HEADS
        return jnp.concatenate([ref[pl.ds(h, page, stride=N_DIFF_HEADS), :] for h in range(N_DIFF_HEADS)],
                               axis=1).astype(BF16)

    k_all = jnp.concatenate([page_rows(k_refs[r]) for r in range(n_pp)], axis=0)
    v_all = jnp.concatenate([page_rows(v_refs[r]) for r in range(n_pp)], axis=0)
    update(lax.dot_general(qbd, k_all, _NT, preferred_element_type=F32), v_all)

    @pl.when(p == pl.num_programs(1) - 1)
    def _():
        pad = jnp.zeros((LANES - t_len, width), F32)
        kn = jnp.concatenate([kn_ref[...], pad], axis=0).astype(BF16)
        vn = jnp.concatenate([vn_ref[...], pad], axis=0).astype(BF16)
        sn = lax.dot_general(qbd, kn, _NT, preferred_element_type=F32)
        qpos = lax.broadcasted_iota(jnp.int32, sn.shape, 0) % t_len
        kpos = lax.broadcasted_iota(jnp.int32, sn.shape, 1)
        sn = jnp.where(kpos <= qpos, sn, NEG_INF)
        update(sn, vn)
        o = acc_ref[...] / l_ref[...]
        lam = _lambda(lq1_ref, lk1_ref, lq2_ref, lk2_ref, lam_init)
        dvw = width // N_DIFF_HEADS
        for h in range(N_DIFF_HEADS):
            cs = slice(h * dvw, (h + 1) * dvw)
            o1 = o[(2 * h) * t_len:(2 * h + 1) * t_len, cs]
            o2 = o[(2 * h + 1) * t_len:(2 * h + 2) * t_len, cs]
            o_ref[:, cs] = _diff_out(o1 - lam * o2, dn_ref[...], lam_init).astype(o_ref.dtype)


def _diff_attn_sample(dq, dk, dv, cache_k, cache_v, page_table, lam_params, diff_norm, t_len, lam_init, layer):
    n, hw = dq.shape
    hd = hw // (2 * N_DIFF_HEADS)
    nseq, n_pages = page_table.shape
    depth, n_pool, page = cache_k.shape[:3]
    page_rows, w = page * N_DIFF_HEADS, hw // N_DIFF_HEADS
    ck = cache_k.reshape(depth * n_pool, page_rows, w)
    cv = cache_v.reshape(depth * n_pool, page_rows, w)
    page_table = page_table + layer * n_pool
    n_pp = math.gcd(PAGES_PER_STEP, n_pages)
    n_rows = 2 * N_DIFF_HEADS * t_len
    row = pl.BlockSpec((t_len, hw), lambda b, p, pt: (b, 0))
    const = lambda shape: pl.BlockSpec(shape, lambda b, p, pt: (0, 0))
    page_spec = lambda r: pl.BlockSpec((None, page_rows, w), lambda b, p, pt: (pt[b, p * n_pp + r], 0, 0))
    grid_spec = pltpu.PrefetchScalarGridSpec(
        num_scalar_prefetch=1,
        grid=(nseq, n_pages // n_pp),
        in_specs=[row, row, row] + [const((1, hd))] * 4 + [const((1, 2 * hd))]
        + [page_spec(r) for r in range(n_pp)] * 2,
        out_specs=row,
        scratch_shapes=[pltpu.VMEM((n_rows, hw), BF16), pltpu.VMEM((n_rows, 1), F32),
                        pltpu.VMEM((n_rows, 1), F32), pltpu.VMEM((n_rows, hw), F32)],
    )
    return pl.pallas_call(
        functools.partial(_paged_kernel, n_pp=n_pp, t_len=t_len, hd=hd, lam_init=lam_init),
        grid_spec=grid_spec,
        out_shape=jax.ShapeDtypeStruct((n, hw), F32),
        compiler_params=_cparams("parallel", "arbitrary"),
        name="diff_attn_sample",
    )(page_table, dq, dk, dv, *lam_params, diff_norm, *([ck] * n_pp), *([cv] * n_pp))


def _merge_kernel(x_ref, ar_ref, ad_ref, gr_ref, gd_ref, wr_ref, wd_ref, wo_ref, o_ref):
    y_ret = jnp.dot(ar_ref[...].astype(BF16), wr_ref[...], preferred_element_type=F32)
    y_diff = jnp.dot(ad_ref[...].astype(BF16), wd_ref[...], preferred_element_type=F32)
    merged = gr_ref[...].astype(F32) * y_ret + gd_ref[...].astype(F32) * y_diff
    o_ref[...] = x_ref[...] + jnp.dot(merged.astype(BF16), wo_ref[...], preferred_element_type=F32)


def _merge(x, a_ret, a_diff, gr, gd, wr, wd, wo):
    n, d = x.shape
    tm = min(MERGE_TM, n)
    row = pl.BlockSpec((tm, d), lambda i: (i, 0))
    return pl.pallas_call(
        _merge_kernel,
        grid=(n // tm,),
        in_specs=[row] * 5 + [_resident(wr.shape), _resident(wd.shape), _resident(wo.shape)],
        out_specs=row,
        out_shape=jax.ShapeDtypeStruct((n, d), F32),
        compiler_params=_cparams("parallel"),
        name="merge",
    )(x, a_ret, a_diff, gr, gd, wr, wd, wo)


def _rotary_tables(pos, dk, hd):
    def cs(half):
        inv = ROPE_THETA ** (-jnp.arange(half, dtype=F32) / half)
        ang = pos[:, None] * inv[None, :]
        return jnp.cos(ang), jnp.sin(ang)

    cr, sr = cs(dk // 2)
    cd, sd = cs(hd // 2)
    zd = jnp.zeros_like(sd)
    reps = LANES // hd
    return (jnp.concatenate([cr, cr], axis=1), jnp.concatenate([-sr, sr], axis=1),
            jnp.tile(cd, (1, 2 * reps)), jnp.tile(jnp.concatenate([-sd, zd], axis=1), (1, reps)),
            jnp.tile(jnp.concatenate([zd, sd], axis=1), (1, reps)))


def kernel(x_prompt, x_sample, cache_k, cache_v, state_ret, page_table, norm_ffn1, ffn1_w1, ffn1_w3, ffn1_w2,
           norm_mix, w_in, ret_norm, w_ret_proj, q_norm, k_norm, lambda_q1, lambda_k1, lambda_q2, lambda_k2,
           diff_norm, w_diff_proj, w_o, norm_ffn2, ffn2_w1, ffn2_w3, ffn2_w2):
    B, S, D = x_prompt.shape
    DB, T, _ = x_sample.shape
    depth = w_in.shape[0]
    hd = D // (2 * N_DIFF_HEADS)
    dk = D // (2 * N_RET_HEADS)
    assert dk == LANES and 2 * hd == LANES, "head layouts assume 128-lane retention heads and 64-lane diff heads"
    assert D % MXU_DIM == 0 and ffn1_w1.shape[-1] % FFN_CHUNK == 0
    past_len = page_table.shape[1] * cache_k.shape[2]

    tabs_p = _rotary_tables(jnp.arange(S, dtype=F32), dk, hd)
    mix_rows_s = min(MIX_TM, DB * T)
    tabs_s = tuple(jnp.tile(t, (mix_rows_s // T, 1))
                   for t in _rotary_tables(past_len + jnp.arange(T, dtype=F32), dk, hd))
    blk = np.arange(MXU_DIM) // hd
    bd = jnp.asarray(blk[:, None] == blk[None, :], dtype=BF16)

    xp = x_prompt.reshape(B * S, D)
    xs = x_sample.reshape(DB * T, D)
    outs = [[] for _ in range(6)]
    for l in range(depth):
        lam_init = 0.8 - 0.6 * math.exp(-0.3 * l)
        row = lambda a: a[l].reshape(1, -1)
        ffn1 = (row(norm_ffn1), ffn1_w1[l].astype(BF16), ffn1_w3[l].astype(BF16), ffn1_w2[l].astype(BF16))
        ffn2 = (row(norm_ffn2), ffn2_w1[l].astype(BF16), ffn2_w3[l].astype(BF16), ffn2_w2[l].astype(BF16))
        w_in_b = w_in[l].astype(BF16)
        qg = jnp.tile(row(q_norm), (1, LANES // hd))
        kg = jnp.tile(row(k_norm), (1, LANES // hd))
        lam_params = (row(lambda_q1), row(lambda_k1), row(lambda_q2), row(lambda_k2))
        rn, dn = row(ret_norm), row(diff_norm)
        proj = (w_ret_proj[l].astype(BF16), w_diff_proj[l].astype(BF16), w_o[l].astype(BF16))

        x1 = _ffn_half(xp, *ffn1)
        rq, rk, rv, rg, dq, dk32, dkb, dv32, dvb, gr, gd = _mixer_in(
            x1, row(norm_mix), w_in_b, tabs_p, qg, kg, bd, BF16, hd)
        a_ret, s_p = _retention_prompt(rq, rk, rv, rg, rn, B, S)
        a_diff = _diff_attn_prompt(dq, dkb, dvb, lam_params, dn, B, S, lam_init)
        xp = _ffn_half(_merge(x1, a_ret, a_diff, gr, gd, *proj), *ffn2)
        outs[0].append(dk32.reshape(B, S, N_DIFF_HEADS, 2 * hd))
        outs[1].append(dv32.reshape(B, S, N_DIFF_HEADS, 2 * hd))
        outs[2].append(s_p)

        x1 = _ffn_half(xs, *ffn1)
        rq, rk, rv, rg, dq, dk32, _, dv32, _, gr, gd = _mixer_in(
            x1, row(norm_mix), w_in_b, tabs_s, qg, kg, bd, F32, hd)
        a_ret, s_s = _retention_sample(rq, rk, rv, rg, state_ret, rn, T, l)
        a_diff = _diff_attn_sample(dq, dk32, dv32, cache_k, cache_v, page_table, lam_params, dn, T,
                                   lam_init, l)
        xs = _ffn_half(_merge(x1, a_ret, a_diff, gr, gd, *proj), *ffn2)
        outs[3].append(dk32.reshape(DB, T, N_DIFF_HEADS, 2 * hd))
        outs[4].append(dv32.reshape(DB, T, N_DIFF_HEADS, 2 * hd))
        outs[5].append(s_s)

    kp, vp, sp, ks, vs, ss = (jnp.stack(o) for o in outs)
    return xp.reshape(B, S, D), xs.reshape(DB, T, D), kp, vp, sp, ks, vs, ss
```
